```python
import jax, jax.numpy as jnp
from jax import lax
import numpy as np

D_MODEL = 1024
BATCH = 8
SEQ = 8192
DEPTH = 4

N_A = DEPTH // 2
N_B = DEPTH - N_A
MEM_LEN = 256
MEM_HEADS = 4
MEM_HEAD_DIM = D_MODEL // 8
MEM_WIDTH = MEM_HEADS * MEM_HEAD_DIM
CONV_DIM = D_MODEL
CONV_WIDTH = 3
FOX_HEADS = 8
FOX_HEAD_DIM = D_MODEL // FOX_HEADS
FOX_WIDTH = FOX_HEADS * FOX_HEAD_DIM
D_FF = ((8 * D_MODEL // 3 + 255) // 256) * 256
Q_BLOCK = 128
RMS_EPS = 1e-6
NEG_INF = float(np.finfo(np.float32).min)

kernel_name = 'yoco_shortconv_fox_macaron_memory'


def rmsnorm(x, g):
    xf = x.astype(jnp.float32)
    y = xf * lax.rsqrt(jnp.mean(xf * xf, axis=-1, keepdims=True) + RMS_EPS)
    return (y * g.astype(jnp.float32)).astype(x.dtype)


def swiglu(h, w_gate_up, w_down):
    gu = h @ w_gate_up
    gate, up = gu[..., :D_FF], gu[..., D_FF:]
    return (jax.nn.silu(gate) * up) @ w_down


def causal_depthwise_conv(u, w):
    c = u.shape[-1]
    return lax.conv_general_dilated(
        u, w[:, None, :].astype(u.dtype), window_strides=(1,),
        padding=[(CONV_WIDTH - 1, 0)],
        dimension_numbers=('NWC', 'WIO', 'NWC'),
        feature_group_count=c)


def memory_attention(q, mk, mv):
    b, s, h, dh = q.shape
    logits = jnp.einsum('bshd,bmhd->bhsm', q, mk,
                        preferred_element_type=jnp.float32) * (dh ** -0.5)
    p = jax.nn.softmax(logits, axis=-1).astype(mv.dtype)
    return jnp.einsum('bhsm,bmhd->bshd', p, mv).reshape(b, s, h * dh)


def forgetting_attention(q, k, v, logf_cum):
    b, s, h, dh = q.shape
    nb = s // Q_BLOCK
    qb = q.reshape(b, nb, Q_BLOCK, h, dh).transpose(1, 0, 2, 3, 4)
    cb = logf_cum.reshape(b, h, nb, Q_BLOCK).transpose(2, 0, 1, 3)
    k_pos = jnp.arange(s)
    scale = dh ** -0.5

    def block(args):
        q_blk, c_blk, i = args
        q_pos = i * Q_BLOCK + jnp.arange(Q_BLOCK)
        logits = jnp.einsum('bqhd,bkhd->bhqk', q_blk, k,
                            preferred_element_type=jnp.float32) * scale
        logits = logits + c_blk[..., :, None] - logf_cum[:, :, None, :]
        causal = k_pos[None, :] <= q_pos[:, None]
        logits = jnp.where(causal, logits, NEG_INF)
        p = jax.nn.softmax(logits, axis=-1).astype(v.dtype)
        return jnp.einsum('bhqk,bkhd->bqhd', p, v)

    out = lax.map(block, (qb, cb, jnp.arange(nb)))
    return out.transpose(1, 0, 2, 3, 4).reshape(b, s, h * dh)


def setup_inputs(seed: int = 0) -> dict:
    key = jax.random.key(seed)
    ks = jax.random.split(key, 20)
    f32 = jnp.float32

    def w(k, shape, fan_in):
        return jax.random.normal(k, shape, f32) * (fan_in ** -0.5)

    def gain(k, shape):
        return 1.0 + 0.02 * jax.random.normal(k, shape, f32)

    a_in_cols = 3 * CONV_DIM + MEM_WIDTH
    b_in_cols = FOX_WIDTH + MEM_WIDTH
    return {
        'x': jax.random.normal(ks[0], (BATCH, SEQ, D_MODEL), f32),
        'mem': jax.random.normal(ks[1], (BATCH, MEM_LEN, D_MODEL), f32),
        'ffn_norm': gain(ks[2], (DEPTH, 2, D_MODEL)),
        'ffn_w_gate_up': w(ks[3], (DEPTH, 2, D_MODEL, 2 * D_FF), D_MODEL),
        'ffn_w_down': w(ks[4], (DEPTH, 2, D_FF, D_MODEL), D_FF),
        'mix_norm': gain(ks[5], (DEPTH, D_MODEL)),
        'mem_norm': gain(ks[6], (D_MODEL,)),
        'mem_w_kv': w(ks[7], (DEPTH, D_MODEL, 2 * MEM_WIDTH), D_MODEL),
        'a_w_in': w(ks[8], (N_A, D_MODEL, a_in_cols), D_MODEL),
        'a_conv_w': w(ks[9], (N_A, CONV_WIDTH, CONV_DIM), CONV_WIDTH),
        'a_w_out': w(ks[10], (N_A, CONV_DIM + MEM_WIDTH, D_MODEL), CONV_DIM + MEM_WIDTH),
        'kv_norm': gain(ks[11], (D_MODEL,)),
        'w_kvf': w(ks[12], (D_MODEL, 2 * FOX_WIDTH + FOX_HEADS), D_MODEL),
        'b_f': jax.random.uniform(ks[13], (FOX_HEADS,), f32, minval=1.0, maxval=6.0),
        'b_w_q': w(ks[14], (N_B, D_MODEL, b_in_cols), D_MODEL),
        'b_w_out': w(ks[15], (N_B, FOX_WIDTH + MEM_WIDTH, D_MODEL), FOX_WIDTH + MEM_WIDTH),
        'final_norm': gain(ks[16], (D_MODEL,)),
    }


def reference(x, mem, ffn_norm, ffn_w_gate_up, ffn_w_down, mix_norm, mem_norm,
              mem_w_kv, a_w_in, a_conv_w, a_w_out, kv_norm, w_kvf, b_f,
              b_w_q, b_w_out, final_norm):
    b, s, _ = x.shape
    m = mem.shape[1]
    mem_n = rmsnorm(mem, mem_norm)
    k_sh = v_sh = c_sh = None

    for l in range(DEPTH):
        if l == N_A:
            hkv = rmsnorm(x, kv_norm)
            kvf = hkv @ w_kvf
            k_sh = kvf[..., :FOX_WIDTH].reshape(b, s, FOX_HEADS, FOX_HEAD_DIM)
            v_sh = kvf[..., FOX_WIDTH:2 * FOX_WIDTH].reshape(b, s, FOX_HEADS, FOX_HEAD_DIM)
            f_logit = (kvf[..., 2 * FOX_WIDTH:] + b_f).astype(jnp.float32)
            c_sh = jnp.cumsum(jax.nn.log_sigmoid(f_logit), axis=1).transpose(0, 2, 1)

        x = x + 0.5 * swiglu(rmsnorm(x, ffn_norm[l, 0]), ffn_w_gate_up[l, 0], ffn_w_down[l, 0])

        h = rmsnorm(x, mix_norm[l])
        mkv = mem_n @ mem_w_kv[l]
        mk = mkv[..., :MEM_WIDTH].reshape(b, m, MEM_HEADS, MEM_HEAD_DIM)
        mv = mkv[..., MEM_WIDTH:].reshape(b, m, MEM_HEADS, MEM_HEAD_DIM)
        if l < N_A:
            i = l
            proj = h @ a_w_in[i]
            gate_b = proj[..., :CONV_DIM]
            gate_c = proj[..., CONV_DIM:2 * CONV_DIM]
            u = proj[..., 2 * CONV_DIM:3 * CONV_DIM]
            qm = proj[..., 3 * CONV_DIM:].reshape(b, s, MEM_HEADS, MEM_HEAD_DIM)
            y_tok = gate_b * causal_depthwise_conv(gate_c * u, a_conv_w[i])
            y_mem = memory_attention(qm, mk, mv)
            x = x + jnp.concatenate([y_tok, y_mem], axis=-1) @ a_w_out[i]
        else:
            j = l - N_A
            proj = h @ b_w_q[j]
            q = proj[..., :FOX_WIDTH].reshape(b, s, FOX_HEADS, FOX_HEAD_DIM)
            qm = proj[..., FOX_WIDTH:].reshape(b, s, MEM_HEADS, MEM_HEAD_DIM)
            y_tok = forgetting_attention(q, k_sh, v_sh, c_sh)
            y_mem = memory_attention(qm, mk, mv)
            x = x + jnp.concatenate([y_tok, y_mem], axis=-1) @ b_w_out[j]

        x = x + 0.5 * swiglu(rmsnorm(x, ffn_norm[l, 1]), ffn_w_gate_up[l, 1], ffn_w_down[l, 1])

    return rmsnorm(x, final_norm)
```

```python
import functools
import math

import jax
import jax.numpy as jnp
import numpy as np
from jax import lax
from jax.experimental import pallas as pl
from jax.experimental.pallas import tpu as pltpu

D_MODEL = 1024
DEPTH = 4
N_A = DEPTH // 2
MEM_LEN = 256
MEM_HEADS = 4
HEAD_DIM = 128
MEM_WIDTH = MEM_HEADS * HEAD_DIM
CONV_DIM = D_MODEL
CONV_WIDTH = 3
FOX_HEADS = 8
FOX_WIDTH = FOX_HEADS * HEAD_DIM
D_FF = 2816
RMS_EPS = 1e-6
NEG_INF = float(np.finfo(np.float32).min)
ATTN_SCALE = HEAD_DIM ** -0.5
LOG2E = math.log2(math.e)

V7X_LANES = 128
V7X_SUBLANES = 8
V7X_MXU_DIM = 256
V7X_VMEM_LIMIT_BYTES = 60000 * 1024

TOKEN_TILE = 512
FF_CHUNK = V7X_MXU_DIM
ATTN_TILE = 512
N_SPLIT = 3

F32 = jnp.float32
BF16 = jnp.bfloat16


def _dot(a, b):
    return jnp.dot(a, b, preferred_element_type=F32)


def _rmsnorm(x, g):
    y = x * lax.rsqrt(jnp.mean(x * x, axis=-1, keepdims=True) + RMS_EPS)
    return y * g


def _split_bf16(x):
    pieces = []
    r = x
    for _ in range(N_SPLIT):
        p = r.astype(BF16)
        pieces.append(p)
        r = r - p.astype(F32)
    return pieces


def _params(n_grid_dims):
    return pltpu.CompilerParams(
        dimension_semantics=("arbitrary",) * n_grid_dims,
        vmem_limit_bytes=V7X_VMEM_LIMIT_BYTES)


def _resident(shape):
    zeros = (0,) * len(shape)
    return pl.BlockSpec(shape, lambda *_: zeros, pipeline_mode=pl.Buffered(1))


def _mem_kv_kernel(mem_ref, g_ref, w_ref, mkT_ref, mv_ref, *, batch):
    h = _rmsnorm(mem_ref[...], g_ref[...]).astype(BF16)
    mkv = _dot(h, w_ref[0])
    for b in range(batch):
        blk = mkv[b * MEM_LEN:(b + 1) * MEM_LEN]
        mkT_ref[0, b] = blk[:, :MEM_WIDTH].T.astype(BF16)
        mv_ref[0, b] = blk[:, MEM_WIDTH:].astype(BF16)


def _mem_kv(mem2d, mem_norm, w_kv, batch):
    rows = mem2d.shape[0]
    return pl.pallas_call(
        functools.partial(_mem_kv_kernel, batch=batch),
        grid=(DEPTH,),
        in_specs=[
            pl.BlockSpec((rows, D_MODEL), lambda l: (0, 0)),
            pl.BlockSpec((1, D_MODEL), lambda l: (0, 0)),
            pl.BlockSpec((1, D_MODEL, 2 * MEM_WIDTH), lambda l: (l, 0, 0)),
        ],
        out_specs=[
            pl.BlockSpec((1, batch, MEM_WIDTH, MEM_LEN), lambda l: (l, 0, 0, 0)),
            pl.BlockSpec((1, batch, MEM_LEN, MEM_WIDTH), lambda l: (l, 0, 0, 0)),
        ],
        out_shape=[
            jax.ShapeDtypeStruct((DEPTH, batch, MEM_WIDTH, MEM_LEN), BF16),
            jax.ShapeDtypeStruct((DEPTH, batch, MEM_LEN, MEM_WIDTH), BF16),
        ],
        compiler_params=_params(1),
        name="mem_kv",
    )(mem2d, mem_norm, w_kv)


def _mem_attn(qm, mkT_ref, mv_ref):
    outs = []
    for hh in range(MEM_HEADS):
        sl = slice(hh * HEAD_DIM, (hh + 1) * HEAD_DIM)
        s = _dot(qm[:, sl].astype(BF16), mkT_ref[0, 0, sl, :]) * ATTN_SCALE
        p = jnp.exp(s - jnp.max(s, axis=-1, keepdims=True))
        denom = jnp.sum(p, axis=-1, keepdims=True)
        outs.append(_dot(p.astype(BF16), mv_ref[0, 0, :, sl]) / denom)
    return jnp.concatenate(outs, axis=-1).astype(BF16)


def _ffn_kernel(x_ref, g_ref, wgu_ref, wd_ref, fg_ref, o_ref, act_ref, *, final):
    x = x_ref[...]
    h = _rmsnorm(x, g_ref[...]).astype(BF16)
    for c in range(D_FF // FF_CHUNK):
        lo = c * FF_CHUNK
        gate = _dot(h, wgu_ref[:, lo:lo + FF_CHUNK])
        up = _dot(h, wgu_ref[:, D_FF + lo:D_FF + lo + FF_CHUNK])
        act_ref[:, lo:lo + FF_CHUNK] = (gate * jax.nn.sigmoid(gate) * up).astype(BF16)
    y = x + 0.5 * _dot(act_ref[...], wd_ref[...])
    if final:
        y = _rmsnorm(y, fg_ref[...])
    o_ref[...] = y


def _ffn(x2d, g, wgu, wd, fg, final):
    tokens = x2d.shape[0]
    tm = TOKEN_TILE
    row = lambda i: (i, 0)
    return pl.pallas_call(
        functools.partial(_ffn_kernel, final=final),
        grid=(tokens // tm,),
        in_specs=[
            pl.BlockSpec((tm, D_MODEL), row),
            _resident((1, D_MODEL)),
            _resident((D_MODEL, 2 * D_FF)),
            _resident((D_FF, D_MODEL)),
            _resident((1, D_MODEL)),
        ],
        out_specs=pl.BlockSpec((tm, D_MODEL), row),
        out_shape=jax.ShapeDtypeStruct((tokens, D_MODEL), F32),
        scratch_shapes=[pltpu.VMEM((tm, D_FF), BF16)],
        compiler_params=_params(1),
        name="ffn_final" if final else "ffn",
    )(x2d, g, wgu, wd, fg)


def _mixer_a_kernel(x_ref, g_ref, win_ref, cw_ref, wout_ref, mkT_ref, mv_ref,
                    o_ref, tail_ref, *, tiles_per_seq):
    tm = x_ref.shape[0]
    c = CONV_DIM

    @pl.when(pl.program_id(0) % tiles_per_seq == 0)
    def _():
        tail_ref[...] = jnp.zeros_like(tail_ref)

    x = x_ref[...]
    h = _rmsnorm(x, g_ref[...]).astype(BF16)
    v = _dot(h, win_ref[:, c:2 * c]) * _dot(h, win_ref[:, 2 * c:3 * c])

    prev = tail_ref[...]
    sub = lax.broadcasted_iota(jnp.int32, (V7X_SUBLANES, c), 0)
    shifted = []
    for k in range(1, CONV_WIDTH):
        rv = pltpu.roll(v, k, 0)
        head = jnp.where(sub < k, pltpu.roll(prev, k, 0), rv[:V7X_SUBLANES])
        shifted.append(jnp.concatenate([head, rv[V7X_SUBLANES:]], axis=0))
    tail_ref[...] = v[tm - V7X_SUBLANES:]

    cw = cw_ref[...]
    conv = cw[CONV_WIDTH - 1:CONV_WIDTH] * v
    for k in range(1, CONV_WIDTH):
        conv = conv + cw[CONV_WIDTH - 1 - k:CONV_WIDTH - k] * shifted[k - 1]
    y_tok = (_dot(h, win_ref[:, :c]) * conv).astype(BF16)

    y_mem = _mem_attn(_dot(h, win_ref[:, 3 * c:]), mkT_ref, mv_ref)
    o_ref[...] = x + _dot(y_tok, wout_ref[:c, :]) + _dot(y_mem, wout_ref[c:, :])


def _mixer_a(x2d, g, w_in, conv_w, w_out, mkT, mv, layer, seq):
    tokens = x2d.shape[0]
    tm = TOKEN_TILE
    tps = seq // tm
    row = lambda i: (i, 0)
    return pl.pallas_call(
        functools.partial(_mixer_a_kernel, tiles_per_seq=tps),
        grid=(tokens // tm,),
        in_specs=[
            pl.BlockSpec((tm, D_MODEL), row),
            _resident((1, D_MODEL)),
            _resident(w_in.shape),
            _resident(conv_w.shape),
            _resident(w_out.shape),
            pl.BlockSpec((1, 1, MEM_WIDTH, MEM_LEN), lambda i: (layer, i // tps, 0, 0)),
            pl.BlockSpec((1, 1, MEM_LEN, MEM_WIDTH), lambda i: (layer, i // tps, 0, 0)),
        ],
        out_specs=pl.BlockSpec((tm, D_MODEL), row),
        out_shape=jax.ShapeDtypeStruct((tokens, D_MODEL), F32),
        scratch_shapes=[pltpu.VMEM((V7X_SUBLANES, CONV_DIM), F32)],
        compiler_params=_params(1),
        name="mixer_a",
    )(x2d, g, w_in, conv_w, w_out, mkT, mv)


def _kvf_kernel(x_ref, g_ref, wkv_ref, wf_ref, bf_ref, tri_ref, sel_ref,
                kp_ref, vT_ref, carry_ref, *, tiles_per_seq):
    tm = x_ref.shape[0]

    @pl.when(pl.program_id(0) % tiles_per_seq == 0)
    def _():
        carry_ref[...] = jnp.zeros_like(carry_ref)

    h = _rmsnorm(x_ref[...], g_ref[...]).astype(BF16)
    k = _dot(h, wkv_ref[:, :FOX_WIDTH])
    v = _dot(h, wkv_ref[:, FOX_WIDTH:])

    z = _dot(h, wf_ref[...]) + bf_ref[...]
    ls = -(jnp.maximum(-z, 0.0) + jnp.log1p(jnp.exp(-jnp.abs(z))))
    cs = _dot(tri_ref[...], jnp.concatenate(_split_bf16(ls), axis=1))
    cum = carry_ref[0:1, :]
    for p in range(N_SPLIT):
        cum = cum + cs[:, p * V7X_LANES:(p + 1) * V7X_LANES]
    carry_ref[...] = jnp.broadcast_to(cum[tm - 1:tm, :], carry_ref.shape)

    bias = _dot(jnp.concatenate(_split_bf16(-LOG2E * cum), axis=1), sel_ref[...])
    for hh in range(FOX_HEADS):
        sl = slice(hh * HEAD_DIM, (hh + 1) * HEAD_DIM)
        kp_ref[0, hh, :, :HEAD_DIM] = k[:, sl].astype(BF16)
        kp_ref[0, hh, :, HEAD_DIM:] = bias[:, sl].astype(BF16)
        vT_ref[0, hh] = v[:, sl].T.astype(BF16)


def _kvf(x2d, g, w_kv, w_f, b_f, batch, seq):
    tokens = x2d.shape[0]
    tm = TOKEN_TILE
    tps = seq // tm
    tri = jnp.tril(jnp.ones((tm, tm), F32)).astype(BF16)
    sel = np.zeros((N_SPLIT * V7X_LANES, FOX_WIDTH), np.float32)
    for p in range(N_SPLIT):
        for hh in range(FOX_HEADS):
            sel[p * V7X_LANES + hh, hh * HEAD_DIM + p] = 1.0
    sel = jnp.asarray(sel, BF16)
    return pl.pallas_call(
        functools.partial(_kvf_kernel, tiles_per_seq=tps),
        grid=(tokens // tm,),
        in_specs=[
            pl.BlockSpec((tm, D_MODEL), lambda i: (i, 0)),
            _resident((1, D_MODEL)),
            _resident(w_kv.shape),
            _resident(w_f.shape),
            _resident(b_f.shape),
            _resident(tri.shape),
            _resident(sel.shape),
        ],
        out_specs=[
            pl.BlockSpec((1, FOX_HEADS, tm, 2 * HEAD_DIM), lambda i: (i // tps, 0, i % tps, 0)),
            pl.BlockSpec((1, FOX_HEADS, HEAD_DIM, tm), lambda i: (i // tps, 0, 0, i % tps)),
        ],
        out_shape=[
            jax.ShapeDtypeStruct((batch, FOX_HEADS, seq, 2 * HEAD_DIM), BF16),
            jax.ShapeDtypeStruct((batch, FOX_HEADS, HEAD_DIM, seq), BF16),
        ],
        scratch_shapes=[pltpu.VMEM((V7X_SUBLANES, V7X_LANES), F32)],
        compiler_params=_params(1),
        name="kvf",
    )(x2d, g, w_kv, w_f, b_f, tri, sel)


def _qproj_kernel(x_ref, g_ref, wq_ref, mkT_ref, mv_ref, qT_ref, ymem_ref):
    h = _rmsnorm(x_ref[...], g_ref[...]).astype(BF16)
    q = _dot(h, wq_ref[:, :FOX_WIDTH]) * (ATTN_SCALE * LOG2E)
    for hh in range(FOX_HEADS):
        qT_ref[0, hh] = q[:, hh * HEAD_DIM:(hh + 1) * HEAD_DIM].T.astype(BF16)
    ymem_ref[...] = _mem_attn(_dot(h, wq_ref[:, FOX_WIDTH:]), mkT_ref, mv_ref)


def _qproj(x2d, g, w_q, mkT, mv, layer, batch, seq):
    tokens = x2d.shape[0]
    tm = TOKEN_TILE
    tps = seq // tm
    return pl.pallas_call(
        _qproj_kernel,
        grid=(tokens // tm,),
        in_specs=[
            pl.BlockSpec((tm, D_MODEL), lambda i: (i, 0)),
            _resident((1, D_MODEL)),
            _resident(w_q.shape),
            pl.BlockSpec((1, 1, MEM_WIDTH, MEM_LEN), lambda i: (layer, i // tps, 0, 0)),
            pl.BlockSpec((1, 1, MEM_LEN, MEM_WIDTH), lambda i: (layer, i // tps, 0, 0)),
        ],
        out_specs=[
            pl.BlockSpec((1, FOX_HEADS, HEAD_DIM, tm), lambda i: (i // tps, 0, 0, i % tps)),
            pl.BlockSpec((tm, MEM_WIDTH), lambda i: (i, 0)),
        ],
        out_shape=[
            jax.ShapeDtypeStruct((batch, FOX_HEADS, HEAD_DIM, seq), BF16),
            jax.ShapeDtypeStruct((tokens, MEM_WIDTH), BF16),
        ],
        compiler_params=_params(1),
        name="qproj",
    )(x2d, g, w_q, mkT, mv)


def _fox_kernel(qT_ref, kp_ref, vT_ref, o_ref, m_ref, l_ref, acc_ref):
    seq = qT_ref.shape[3]
    t = ATTN_TILE
    ones_rows = (lax.broadcasted_iota(jnp.int32, (HEAD_DIM, t), 0) < N_SPLIT).astype(BF16)
    causal = (lax.broadcasted_iota(jnp.int32, (t, t), 0)
              <= lax.broadcasted_iota(jnp.int32, (t, t), 1))

    def q_tile(qi, carry):
        q0 = pl.multiple_of(qi * t, t)
        qa = jnp.concatenate([qT_ref[0, 0, :, pl.ds(q0, t)], ones_rows], axis=0)
        m_ref[...] = jnp.full_like(m_ref, -jnp.inf)
        l_ref[...] = jnp.zeros_like(l_ref)
        acc_ref[...] = jnp.zeros_like(acc_ref)

        def k_tile(k0, diagonal):
            s = _dot(kp_ref[0, 0, pl.ds(k0, t), :], qa)
            if diagonal:
                s = jnp.where(causal, s, NEG_INF)
            m_prev = m_ref[...]
            m_new = jnp.maximum(m_prev, jnp.max(s, axis=0, keepdims=True))
            alpha = jnp.exp2(m_prev - m_new)
            p = jnp.exp2(s - m_new)
            l_ref[...] = alpha * l_ref[...] + jnp.sum(p, axis=0, keepdims=True)
            acc_ref[...] = alpha * acc_ref[...] + _dot(vT_ref[0, 0, :, pl.ds(k0, t)], p.astype(BF16))
            m_ref[...] = m_new

        def body(ki, c):
            k_tile(pl.multiple_of(ki * t, t), False)
            return c

        lax.fori_loop(0, qi, body, 0)
        k_tile(q0, True)
        o_ref[0, pl.ds(q0, t), :] = (acc_ref[...] / l_ref[...]).T.astype(BF16)
        return carry

    lax.fori_loop(0, seq // t, q_tile, 0)


def _fox_attention(qT, kp, vT):
    batch, heads, _, seq = qT.shape
    t = ATTN_TILE
    return pl.pallas_call(
        _fox_kernel,
        grid=(batch, heads),
        in_specs=[
            pl.BlockSpec((1, 1, HEAD_DIM, seq), lambda b, h: (b, h, 0, 0)),
            pl.BlockSpec((1, 1, seq, 2 * HEAD_DIM), lambda b, h: (b, h, 0, 0)),
            pl.BlockSpec((1, 1, HEAD_DIM, seq), lambda b, h: (b, h, 0, 0)),
        ],
        out_specs=pl.BlockSpec((1, seq, HEAD_DIM), lambda b, h: (b, 0, h)),
        out_shape=jax.ShapeDtypeStruct((batch, seq, heads * HEAD_DIM), BF16),
        scratch_shapes=[
            pltpu.VMEM((1, t), F32),
            pltpu.VMEM((1, t), F32),
            pltpu.VMEM((HEAD_DIM, t), F32),
        ],
        compiler_params=_params(2),
        name="fox_attention",
    )(qT, kp, vT)


def _outproj_kernel(x_ref, yt_ref, ym_ref, w_ref, o_ref):
    o_ref[...] = (x_ref[...] + _dot(yt_ref[...], w_ref[:FOX_WIDTH, :])
                  + _dot(ym_ref[...], w_ref[FOX_WIDTH:, :]))


def _outproj(x2d, y_tok, y_mem, w_out):
    tokens = x2d.shape[0]
    tm = TOKEN_TILE
    row = lambda i: (i, 0)
    return pl.pallas_call(
        _outproj_kernel,
        grid=(tokens // tm,),
        in_specs=[
            pl.BlockSpec((tm, D_MODEL), row),
            pl.BlockSpec((tm, FOX_WIDTH), row),
            pl.BlockSpec((tm, MEM_WIDTH), row),
            _resident(w_out.shape),
        ],
        out_specs=pl.BlockSpec((tm, D_MODEL), row),
        out_shape=jax.ShapeDtypeStruct((tokens, D_MODEL), F32),
        compiler_params=_params(1),
        name="outproj",
    )(x2d, y_tok, y_mem, w_out)


def kernel(x, mem, ffn_norm, ffn_w_gate_up, ffn_w_down, mix_norm, mem_norm, mem_w_kv,
           a_w_in, a_conv_w, a_w_out, kv_norm, w_kvf, b_f, b_w_q, b_w_out, final_norm):
    batch, seq, d = x.shape
    assert d == D_MODEL and seq % TOKEN_TILE == 0 and seq % ATTN_TILE == 0
    assert mem.shape == (batch, MEM_LEN, D_MODEL)
    tokens = batch * seq

    wgu = ffn_w_gate_up.astype(BF16)
    wd = ffn_w_down.astype(BF16)
    w_mem = mem_w_kv.astype(BF16)
    w_in = a_w_in.astype(BF16)
    w_out_a = a_w_out.astype(BF16)
    w_kv = w_kvf[:, :2 * FOX_WIDTH].astype(BF16)
    w_f = jnp.pad(w_kvf[:, 2 * FOX_WIDTH:], ((0, 0), (0, V7X_LANES - FOX_HEADS))).astype(BF16)
    b_f_row = jnp.pad(b_f, (0, V7X_LANES - FOX_HEADS)).reshape(1, V7X_LANES)
    w_q = b_w_q.astype(BF16)
    w_out_b = b_w_out.astype(BF16)
    fg = final_norm.reshape(1, D_MODEL)

    mkT, mv = _mem_kv(mem.reshape(batch * MEM_LEN, D_MODEL), mem_norm.reshape(1, D_MODEL),
                      w_mem, batch)

    h = x.reshape(tokens, D_MODEL)
    kp = vT = None
    for l in range(DEPTH):
        if l == N_A:
            kp, vT = _kvf(h, kv_norm.reshape(1, D_MODEL), w_kv, w_f, b_f_row, batch, seq)
        h = _ffn(h, ffn_norm[l, 0].reshape(1, D_MODEL), wgu[l, 0], wd[l, 0], fg, False)
        g_mix = mix_norm[l].reshape(1, D_MODEL)
        if l < N_A:
            h = _mixer_a(h, g_mix, w_in[l], a_conv_w[l], w_out_a[l], mkT, mv, l, seq)
        else:
            j = l - N_A
            qT, y_mem = _qproj(h, g_mix, w_q[j], mkT, mv, l, batch, seq)
            y_tok = _fox_attention(qT, kp, vT).reshape(tokens, FOX_WIDTH)
            h = _outproj(h, y_tok, y_mem, w_out_b[j])
        h = _ffn(h, ffn_norm[l, 1].reshape(1, D_MODEL), wgu[l, 1], wd[l, 1], fg, l == DEPTH - 1)
    return h.reshape(batch, seq, D_MODEL)
```

```python
import functools
import math

import jax
import jax.numpy as jnp
import numpy as np
from jax import lax
from jax.experimental import pallas as pl
from jax.experimental.pallas import tpu as pltpu

D_MODEL = 1024
DEPTH = 4
N_A = DEPTH // 2
MEM_LEN = 256
MEM_HEADS = 4
HEAD_DIM = 128
MEM_WIDTH = MEM_HEADS * HEAD_DIM
CONV_DIM = D_MODEL
CONV_WIDTH = 3
FOX_HEADS = 8
FOX_WIDTH = FOX_HEADS * HEAD_DIM
D_FF = 2816
RMS_EPS = 1e-6
NEG_INF = float(np.finfo(np.float32).min)
ATTN_SCALE = HEAD_DIM ** -0.5
LOG2E = math.log2(math.e)

V7X_LANES = 128
V7X_SUBLANES = 8
V7X_MXU_DIM = 256
V7X_VMEM_LIMIT_BYTES = 60000 * 1024

TOKEN_TILE = 512
FF_CHUNK = V7X_MXU_DIM
ATTN_TILE = 512
ATTN_KEY_TILE = 256
ATTN_KEY_STEPS = 2
ATTN_LOOKAHEAD = 2
ATTN_HEADS = 8
N_SPLIT = 3
V_PAD_ROWS = 16
V_ROWS = HEAD_DIM + V_PAD_ROWS

F32 = jnp.float32
BF16 = jnp.bfloat16


def _dot(a, b):
    return jnp.dot(a, b, preferred_element_type=F32)


def _rmsnorm(x, g):
    y = x * lax.rsqrt(jnp.mean(x * x, axis=-1, keepdims=True) + RMS_EPS)
    return y * g


def _split_bf16(x):
    pieces = []
    r = x
    for _ in range(N_SPLIT):
        p = r.astype(BF16)
        pieces.append(p)
        r = r - p.astype(F32)
    return pieces


def _params(n_grid_dims):
    return pltpu.CompilerParams(
        dimension_semantics=("arbitrary",) * n_grid_dims,
        vmem_limit_bytes=V7X_VMEM_LIMIT_BYTES)


def _resident(shape):
    zeros = (0,) * len(shape)
    return pl.BlockSpec(shape, lambda *_: zeros, pipeline_mode=pl.Buffered(1))


def _mem_kv_kernel(mem_ref, g_ref, w_ref, mkT_ref, mv_ref, *, batch):
    h = _rmsnorm(mem_ref[...], g_ref[...]).astype(BF16)
    mkv = _dot(h, w_ref[0])
    for b in range(batch):
        blk = mkv[b * MEM_LEN:(b + 1) * MEM_LEN]
        mkT_ref[0, b] = blk[:, :MEM_WIDTH].T.astype(BF16)
        mv_ref[0, b] = blk[:, MEM_WIDTH:].astype(BF16)


def _mem_kv(mem2d, mem_norm, w_kv, batch):
    rows = mem2d.shape[0]
    return pl.pallas_call(
        functools.partial(_mem_kv_kernel, batch=batch),
        grid=(DEPTH,),
        in_specs=[
            pl.BlockSpec((rows, D_MODEL), lambda l: (0, 0)),
            pl.BlockSpec((1, D_MODEL), lambda l: (0, 0)),
            pl.BlockSpec((1, D_MODEL, 2 * MEM_WIDTH), lambda l: (l, 0, 0)),
        ],
        out_specs=[
            pl.BlockSpec((1, batch, MEM_WIDTH, MEM_LEN), lambda l: (l, 0, 0, 0)),
            pl.BlockSpec((1, batch, MEM_LEN, MEM_WIDTH), lambda l: (l, 0, 0, 0)),
        ],
        out_shape=[
            jax.ShapeDtypeStruct((DEPTH, batch, MEM_WIDTH, MEM_LEN), BF16),
            jax.ShapeDtypeStruct((DEPTH, batch, MEM_LEN, MEM_WIDTH), BF16),
        ],
        compiler_params=_params(1),
        name="mem_kv",
    )(mem2d, mem_norm, w_kv)


def _mem_attn(qm, mkT_ref, mv_ref):
    outs = []
    for hh in range(MEM_HEADS):
        sl = slice(hh * HEAD_DIM, (hh + 1) * HEAD_DIM)
        s = _dot(qm[:, sl].astype(BF16), mkT_ref[0, 0, sl, :]) * ATTN_SCALE
        p = jnp.exp(s - jnp.max(s, axis=-1, keepdims=True))
        denom = jnp.sum(p, axis=-1, keepdims=True)
        outs.append(_dot(p.astype(BF16), mv_ref[0, 0, :, sl]) / denom)
    return jnp.concatenate(outs, axis=-1).astype(BF16)


def _ffn_kernel(x_ref, g_ref, wgu_ref, wd_ref, fg_ref, o_ref, act_ref, *, final):
    x = x_ref[...]
    h = _rmsnorm(x, g_ref[...]).astype(BF16)
    for c in range(D_FF // FF_CHUNK):
        lo = c * FF_CHUNK
        gate = _dot(h, wgu_ref[:, lo:lo + FF_CHUNK])
        up = _dot(h, wgu_ref[:, D_FF + lo:D_FF + lo + FF_CHUNK])
        act_ref[:, lo:lo + FF_CHUNK] = (gate * jax.nn.sigmoid(gate) * up).astype(BF16)
    y = x + 0.5 * _dot(act_ref[...], wd_ref[...])
    if final:
        y = _rmsnorm(y, fg_ref[...])
    o_ref[...] = y


def _ffn(x2d, g, wgu, wd, fg, final):
    tokens = x2d.shape[0]
    tm = TOKEN_TILE
    row = lambda i: (i, 0)
    return pl.pallas_call(
        functools.partial(_ffn_kernel, final=final),
        grid=(tokens // tm,),
        in_specs=[
            pl.BlockSpec((tm, D_MODEL), row),
            _resident((1, D_MODEL)),
            _resident((D_MODEL, 2 * D_FF)),
            _resident((D_FF, D_MODEL)),
            _resident((1, D_MODEL)),
        ],
        out_specs=pl.BlockSpec((tm, D_MODEL), row),
        out_shape=jax.ShapeDtypeStruct((tokens, D_MODEL), F32),
        scratch_shapes=[pltpu.VMEM((tm, D_FF), BF16)],
        compiler_params=_params(1),
        name="ffn_final" if final else "ffn",
    )(x2d, g, wgu, wd, fg)


def _mixer_a_kernel(x_ref, g_ref, win_ref, cw_ref, wout_ref, mkT_ref, mv_ref,
                    o_ref, tail_ref, *, tiles_per_seq):
    tm = x_ref.shape[0]
    c = CONV_DIM

    @pl.when(pl.program_id(0) % tiles_per_seq == 0)
    def _():
        tail_ref[...] = jnp.zeros_like(tail_ref)

    x = x_ref[...]
    h = _rmsnorm(x, g_ref[...]).astype(BF16)
    v = _dot(h, win_ref[:, c:2 * c]) * _dot(h, win_ref[:, 2 * c:3 * c])

    prev = tail_ref[...]
    sub = lax.broadcasted_iota(jnp.int32, (V7X_SUBLANES, c), 0)
    shifted = []
    for k in range(1, CONV_WIDTH):
        rv = pltpu.roll(v, k, 0)
        head = jnp.where(sub < k, pltpu.roll(prev, k, 0), rv[:V7X_SUBLANES])
        shifted.append(jnp.concatenate([head, rv[V7X_SUBLANES:]], axis=0))
    tail_ref[...] = v[tm - V7X_SUBLANES:]

    cw = cw_ref[...]
    conv = cw[CONV_WIDTH - 1:CONV_WIDTH] * v
    for k in range(1, CONV_WIDTH):
        conv = conv + cw[CONV_WIDTH - 1 - k:CONV_WIDTH - k] * shifted[k - 1]
    y_tok = (_dot(h, win_ref[:, :c]) * conv).astype(BF16)

    y_mem = _mem_attn(_dot(h, win_ref[:, 3 * c:]), mkT_ref, mv_ref)
    o_ref[...] = x + _dot(y_tok, wout_ref[:c, :]) + _dot(y_mem, wout_ref[c:, :])


def _mixer_a(x2d, g, w_in, conv_w, w_out, mkT, mv, layer, seq):
    tokens = x2d.shape[0]
    tm = TOKEN_TILE
    tps = seq // tm
    row = lambda i: (i, 0)
    return pl.pallas_call(
        functools.partial(_mixer_a_kernel, tiles_per_seq=tps),
        grid=(tokens // tm,),
        in_specs=[
            pl.BlockSpec((tm, D_MODEL), row),
            _resident((1, D_MODEL)),
            _resident(w_in.shape),
            _resident(conv_w.shape),
            _resident(w_out.shape),
            pl.BlockSpec((1, 1, MEM_WIDTH, MEM_LEN), lambda i: (layer, i // tps, 0, 0)),
            pl.BlockSpec((1, 1, MEM_LEN, MEM_WIDTH), lambda i: (layer, i // tps, 0, 0)),
        ],
        out_specs=pl.BlockSpec((tm, D_MODEL), row),
        out_shape=jax.ShapeDtypeStruct((tokens, D_MODEL), F32),
        scratch_shapes=[pltpu.VMEM((V7X_SUBLANES, CONV_DIM), F32)],
        compiler_params=_params(1),
        name="mixer_a",
    )(x2d, g, w_in, conv_w, w_out, mkT, mv)


def _kvf_kernel(x_ref, g_ref, wkv_ref, wf_ref, bf_ref, tri_ref, sel_ref,
                k_ref, bias_ref, vT_ref, carry_ref, *, tiles_per_seq):
    tm = x_ref.shape[0]

    @pl.when(pl.program_id(0) % tiles_per_seq == 0)
    def _():
        carry_ref[...] = jnp.zeros_like(carry_ref)

    h = _rmsnorm(x_ref[...], g_ref[...]).astype(BF16)
    k = _dot(h, wkv_ref[:, :FOX_WIDTH])
    v = _dot(h, wkv_ref[:, FOX_WIDTH:])

    z = _dot(h, wf_ref[...]) + bf_ref[...]
    ls = -(jnp.maximum(-z, 0.0) + jnp.log1p(jnp.exp(-jnp.abs(z))))
    cs = _dot(tri_ref[...], jnp.concatenate(_split_bf16(ls), axis=1))
    cum = carry_ref[0:1, :]
    for p in range(N_SPLIT):
        cum = cum + cs[:, p * V7X_LANES:(p + 1) * V7X_LANES]
    carry_ref[...] = jnp.broadcast_to(cum[tm - 1:tm, :], carry_ref.shape)

    bias = _dot(jnp.concatenate(_split_bf16(-LOG2E * cum), axis=1), sel_ref[...])
    bias_ref[0] = bias.astype(BF16)
    for hh in range(FOX_HEADS):
        sl = slice(hh * HEAD_DIM, (hh + 1) * HEAD_DIM)
        k_ref[0, hh] = k[:, sl].astype(BF16)
        vT_ref[0, hh, :HEAD_DIM, :] = v[:, sl].T.astype(BF16)
    denom_rows = lax.broadcasted_iota(jnp.int32, (V_PAD_ROWS, tm), 0) == 0
    for hh in range(FOX_HEADS):
        vT_ref[0, hh, HEAD_DIM:, :] = jnp.where(denom_rows, 1.0, 0.0).astype(BF16)


def _kvf(x2d, g, w_kv, w_f, b_f, batch, seq):
    tokens = x2d.shape[0]
    tm = TOKEN_TILE
    tps = seq // tm
    tri = jnp.tril(jnp.ones((tm, tm), F32)).astype(BF16)
    sel = np.zeros((N_SPLIT * V7X_LANES, V7X_LANES), np.float32)
    for p in range(N_SPLIT):
        for hh in range(FOX_HEADS):
            sel[p * V7X_LANES + hh, N_SPLIT * hh + p] = 1.0
    sel = jnp.asarray(sel, BF16)
    return pl.pallas_call(
        functools.partial(_kvf_kernel, tiles_per_seq=tps),
        grid=(tokens // tm,),
        in_specs=[
            pl.BlockSpec((tm, D_MODEL), lambda i: (i, 0)),
            _resident((1, D_MODEL)),
            _resident(w_kv.shape),
            _resident(w_f.shape),
            _resident(b_f.shape),
            _resident(tri.shape),
            _resident(sel.shape),
        ],
        out_specs=[
            pl.BlockSpec((1, FOX_HEADS, tm, HEAD_DIM), lambda i: (i // tps, 0, i % tps, 0)),
            pl.BlockSpec((1, tm, V7X_LANES), lambda i: (i // tps, i % tps, 0)),
            pl.BlockSpec((1, FOX_HEADS, V_ROWS, tm), lambda i: (i // tps, 0, 0, i % tps)),
        ],
        out_shape=[
            jax.ShapeDtypeStruct((batch, FOX_HEADS, seq, HEAD_DIM), BF16),
            jax.ShapeDtypeStruct((batch, seq, V7X_LANES), BF16),
            jax.ShapeDtypeStruct((batch, FOX_HEADS, V_ROWS, seq), BF16),
        ],
        scratch_shapes=[pltpu.VMEM((V7X_SUBLANES, V7X_LANES), F32)],
        compiler_params=_params(1),
        name="kvf",
    )(x2d, g, w_kv, w_f, b_f, tri, sel)


def _qproj_kernel(x_ref, g_ref, wq_ref, mkT_ref, mv_ref, qT_ref, ymem_ref):
    h = _rmsnorm(x_ref[...], g_ref[...]).astype(BF16)
    q = _dot(h, wq_ref[:, :FOX_WIDTH]) * (ATTN_SCALE * LOG2E)
    for hh in range(FOX_HEADS):
        qT_ref[0, hh] = q[:, hh * HEAD_DIM:(hh + 1) * HEAD_DIM].T.astype(BF16)
    ymem_ref[...] = _mem_attn(_dot(h, wq_ref[:, FOX_WIDTH:]), mkT_ref, mv_ref)


def _qproj(x2d, g, w_q, mkT, mv, layer, batch, seq):
    tokens = x2d.shape[0]
    tm = TOKEN_TILE
    tps = seq // tm
    return pl.pallas_call(
        _qproj_kernel,
        grid=(tokens // tm,),
        in_specs=[
            pl.BlockSpec((tm, D_MODEL), lambda i: (i, 0)),
            _resident((1, D_MODEL)),
            _resident(w_q.shape),
            pl.BlockSpec((1, 1, MEM_WIDTH, MEM_LEN), lambda i: (layer, i // tps, 0, 0)),
            pl.BlockSpec((1, 1, MEM_LEN, MEM_WIDTH), lambda i: (layer, i // tps, 0, 0)),
        ],
        out_specs=[
            pl.BlockSpec((1, FOX_HEADS, HEAD_DIM, tm), lambda i: (i // tps, 0, 0, i % tps)),
            pl.BlockSpec((tm, MEM_WIDTH), lambda i: (i, 0)),
        ],
        out_shape=[
            jax.ShapeDtypeStruct((batch, FOX_HEADS, HEAD_DIM, seq), BF16),
            jax.ShapeDtypeStruct((tokens, MEM_WIDTH), BF16),
        ],
        compiler_params=_params(1),
        name="qproj",
    )(x2d, g, w_q, mkT, mv)


def _fox_kernel(qT_ref, k_ref, bias_ref, vT_ref, o_ref, m_ref, acc_ref, s_ref):
    t = ATTN_TILE
    tk = ATTN_KEY_TILE
    heads = qT_ref.shape[1]
    qi = pl.program_id(2)
    head0 = pl.program_id(1) * heads
    q0 = pl.multiple_of(qi * t, t)
    row = lax.broadcasted_iota(jnp.int32, (HEAD_DIM, t), 0)

    qa = []
    for g in range(heads):
        lo = N_SPLIT * (head0 + g)
        ones_rows = jnp.where((row >= lo) & (row < lo + N_SPLIT), 1.0, 0.0).astype(BF16)
        qa.append(jnp.concatenate([qT_ref[0, g], ones_rows], axis=0))
    m_ref[...] = jnp.full_like(m_ref, -jnp.inf)
    acc_ref[...] = jnp.zeros_like(acc_ref)

    def logits(unit):
        g, k0, _ = unit
        ka = jnp.concatenate([k_ref[0, g, pl.ds(k0, tk), :], bias_ref[0, pl.ds(k0, tk), :]],
                             axis=1)
        return _dot(ka, qa[g])

    def softmax_step(unit, s):
        g, k0, diag_offset = unit
        if diag_offset is not None:
            causal = (lax.broadcasted_iota(jnp.int32, (tk, t), 0) + diag_offset
                      <= lax.broadcasted_iota(jnp.int32, (tk, t), 1))
            s = jnp.where(causal, s, NEG_INF)
        m_prev = m_ref[g]
        m_new = jnp.maximum(m_prev, jnp.max(s, axis=0, keepdims=True))
        alpha = jnp.exp2(m_prev - m_new)
        p = jnp.exp2(s - m_new)
        acc_ref[g] = alpha * acc_ref[g] + _dot(vT_ref[0, g, :, pl.ds(k0, tk)], p.astype(BF16))
        m_ref[g] = m_new

    def first_units(k0):
        return [(g, k0, None) for g in range(ATTN_LOOKAHEAD)]

    def run(units, following):
        pending = [s_ref[j] for j in range(ATTN_LOOKAHEAD)]
        seq = units + following
        for i, u in enumerate(units):
            s = pending.pop(0)
            if i + ATTN_LOOKAHEAD < len(seq):
                pending.append(logits(seq[i + ATTN_LOOKAHEAD]))
            softmax_step(u, s)
        for j, s in enumerate(pending):
            s_ref[j] = s

    def body(ki, c):
        base = ki * (ATTN_KEY_STEPS * tk)
        run([(g, pl.multiple_of(base + j * tk, tk), None)
             for j in range(ATTN_KEY_STEPS) for g in range(heads)],
            first_units(pl.multiple_of(base + ATTN_KEY_STEPS * tk, tk)))
        return c

    for j, u in enumerate(first_units(0)):
        s_ref[j] = logits(u)
    lax.fori_loop(0, qi * (t // (ATTN_KEY_STEPS * tk)), body, 0)
    run([(g, pl.multiple_of(q0 + d * tk, tk), d * tk)
         for d in range(t // tk) for g in range(heads)], [])
    for g in range(heads):
        out = acc_ref[g, :HEAD_DIM, :] / acc_ref[g, HEAD_DIM:HEAD_DIM + 1, :]
        o_ref[0, :, g * HEAD_DIM:(g + 1) * HEAD_DIM] = out.T.astype(BF16)


def _fox_attention(qT, k, bias, vT):
    batch, heads, _, seq = qT.shape
    t = ATTN_TILE
    hg = ATTN_HEADS
    assert ATTN_LOOKAHEAD <= hg and t % (ATTN_KEY_STEPS * ATTN_KEY_TILE) == 0
    once = pl.Buffered(1)
    return pl.pallas_call(
        _fox_kernel,
        grid=(batch, heads // hg, seq // t),
        in_specs=[
            pl.BlockSpec((1, hg, HEAD_DIM, t), lambda b, h, q: (b, h, 0, q)),
            pl.BlockSpec((1, hg, seq, HEAD_DIM), lambda b, h, q: (b, h, 0, 0), pipeline_mode=once),
            pl.BlockSpec((1, seq, V7X_LANES), lambda b, h, q: (b, 0, 0), pipeline_mode=once),
            pl.BlockSpec((1, hg, V_ROWS, seq), lambda b, h, q: (b, h, 0, 0), pipeline_mode=once),
        ],
        out_specs=pl.BlockSpec((1, t, hg * HEAD_DIM), lambda b, h, q: (b, q, h)),
        out_shape=jax.ShapeDtypeStruct((batch, seq, heads * HEAD_DIM), BF16),
        scratch_shapes=[
            pltpu.VMEM((hg, 1, t), F32),
            pltpu.VMEM((hg, V_ROWS, t), F32),
            pltpu.VMEM((ATTN_LOOKAHEAD, ATTN_KEY_TILE, t), F32),
        ],
        compiler_params=_params(3),
        name="fox_attention",
    )(qT, k, bias, vT)


def _outproj_kernel(x_ref, yt_ref, ym_ref, w_ref, o_ref):
    o_ref[...] = (x_ref[...] + _dot(yt_ref[...], w_ref[:FOX_WIDTH, :])
                  + _dot(ym_ref[...], w_ref[FOX_WIDTH:, :]))


def _outproj(x2d, y_tok, y_mem, w_out):
    tokens = x2d.shape[0]
    tm = TOKEN_TILE
    row = lambda i: (i, 0)
    return pl.pallas_call(
        _outproj_kernel,
        grid=(tokens // tm,),
        in_specs=[
            pl.BlockSpec((tm, D_MODEL), row),
            pl.BlockSpec((tm, FOX_WIDTH), row),
            pl.BlockSpec((tm, MEM_WIDTH), row),
            _resident(w_out.shape),
        ],
        out_specs=pl.BlockSpec((tm, D_MODEL), row),
        out_shape=jax.ShapeDtypeStruct((tokens, D_MODEL), F32),
        compiler_params=_params(1),
        name="outproj",
    )(x2d, y_tok, y_mem, w_out)


def kernel(x, mem, ffn_norm, ffn_w_gate_up, ffn_w_down, mix_norm, mem_norm, mem_w_kv,
           a_w_in, a_conv_w, a_w_out, kv_norm, w_kvf, b_f, b_w_q, b_w_out, final_norm):
    batch, seq, d = x.shape
    assert d == D_MODEL and seq % TOKEN_TILE == 0 and seq % ATTN_TILE == 0
    assert mem.shape == (batch, MEM_LEN, D_MODEL)
    tokens = batch * seq

    wgu = ffn_w_gate_up.astype(BF16)
    wd = ffn_w_down.astype(BF16)
    w_mem = mem_w_kv.astype(BF16)
    w_in = a_w_in.astype(BF16)
    w_out_a = a_w_out.astype(BF16)
    w_kv = w_kvf[:, :2 * FOX_WIDTH].astype(BF16)
    w_f = jnp.pad(w_kvf[:, 2 * FOX_WIDTH:], ((0, 0), (0, V7X_LANES - FOX_HEADS))).astype(BF16)
    b_f_row = jnp.pad(b_f, (0, V7X_LANES - FOX_HEADS)).reshape(1, V7X_LANES)
    w_q = b_w_q.astype(BF16)
    w_out_b = b_w_out.astype(BF16)
    fg = final_norm.reshape(1, D_MODEL)

    mkT, mv = _mem_kv(mem.reshape(batch * MEM_LEN, D_MODEL), mem_norm.reshape(1, D_MODEL),
                      w_mem, batch)

    h = x.reshape(tokens, D_MODEL)
    k_sh = bias_sh = vT = None
    for l in range(DEPTH):
        if l == N_A:
            k_sh, bias_sh, vT = _kvf(h, kv_norm.reshape(1, D_MODEL), w_kv, w_f, b_f_row, batch, seq)
        h = _ffn(h, ffn_norm[l, 0].reshape(1, D_MODEL), wgu[l, 0], wd[l, 0], fg, False)
        g_mix = mix_norm[l].reshape(1, D_MODEL)
        if l < N_A:
            h = _mixer_a(h, g_mix, w_in[l], a_conv_w[l], w_out_a[l], mkT, mv, l, seq)
        else:
            j = l - N_A
            qT, y_mem = _qproj(h, g_mix, w_q[j], mkT, mv, l, batch, seq)
            y_tok = _fox_attention(qT, k_sh, bias_sh, vT).reshape(tokens, FOX_WIDTH)
            h = _outproj(h, y_tok, y_mem, w_out_b[j])
        h = _ffn(h, ffn_norm[l, 1].reshape(1, D_MODEL), wgu[l, 1], wd[l, 1], fg, l == DEPTH - 1)
    return h.reshape(batch, seq, D_MODEL)
```

```python
import functools
import math

import jax
import jax.numpy as jnp
import numpy as np
from jax import lax
from jax.experimental import pallas as pl
from jax.experimental.pallas import tpu as pltpu

D_MODEL = 1024
DEPTH = 4
N_A = DEPTH // 2
MEM_LEN = 256
MEM_HEADS = 4
HEAD_DIM = 128
MEM_WIDTH = MEM_HEADS * HEAD_DIM
CONV_DIM = D_MODEL
CONV_WIDTH = 3
FOX_HEADS = 8
FOX_WIDTH = FOX_HEADS * HEAD_DIM
D_FF = 2816
RMS_EPS = 1e-6
NEG_INF = float(np.finfo(np.float32).min)
ATTN_SCALE = HEAD_DIM ** -0.5
LOG2E = math.log2(math.e)

V7X_LANES = 128
V7X_SUBLANES = 8
V7X_MXU_DIM = 256
V7X_VMEM_LIMIT_BYTES = 60000 * 1024

TOKEN_TILE = 512
FF_CHUNK = V7X_MXU_DIM
ATTN_TILE = 512
ATTN_KEY_TILE = 256
ATTN_KEY_STEPS = 2
ATTN_LOOKAHEAD = 2
ATTN_HEADS = 8
N_SPLIT = 3
V_PAD_ROWS = 16
V_ROWS = HEAD_DIM + V_PAD_ROWS
AUG_SHIFT = N_SPLIT * FOX_HEADS
AUG_ROWS = 32
assert AUG_SHIFT + N_SPLIT <= AUG_ROWS <= HEAD_DIM
BOUND_MARGIN = 1.02
MAX_SHIFT_GAP = 100.0

F32 = jnp.float32
BF16 = jnp.bfloat16


def _dot(a, b):
    return jnp.dot(a, b, preferred_element_type=F32)


def _rmsnorm(x, g):
    y = x * lax.rsqrt(jnp.mean(x * x, axis=-1, keepdims=True) + RMS_EPS)
    return y * g


def _split_bf16(x):
    pieces = []
    r = x
    for _ in range(N_SPLIT):
        p = r.astype(BF16)
        pieces.append(p)
        r = r - p.astype(F32)
    return pieces


def _head_sum_matrix():
    m = np.zeros((FOX_WIDTH, V7X_LANES), np.float32)
    for hh in range(FOX_HEADS):
        m[hh * HEAD_DIM:(hh + 1) * HEAD_DIM, hh] = 1.0
    return jnp.asarray(m, BF16)


def _params(n_grid_dims):
    return pltpu.CompilerParams(
        dimension_semantics=("arbitrary",) * n_grid_dims,
        vmem_limit_bytes=V7X_VMEM_LIMIT_BYTES)


def _resident(shape):
    zeros = (0,) * len(shape)
    return pl.BlockSpec(shape, lambda *_: zeros, pipeline_mode=pl.Buffered(1))


def _mem_kv_kernel(mem_ref, g_ref, w_ref, mkT_ref, mv_ref, *, batch):
    h = _rmsnorm(mem_ref[...], g_ref[...]).astype(BF16)
    mkv = _dot(h, w_ref[0])
    for b in range(batch):
        blk = mkv[b * MEM_LEN:(b + 1) * MEM_LEN]
        mkT_ref[0, b] = blk[:, :MEM_WIDTH].T.astype(BF16)
        mv_ref[0, b] = blk[:, MEM_WIDTH:].astype(BF16)


def _mem_kv(mem2d, mem_norm, w_kv, batch):
    rows = mem2d.shape[0]
    return pl.pallas_call(
        functools.partial(_mem_kv_kernel, batch=batch),
        grid=(DEPTH,),
        in_specs=[
            pl.BlockSpec((rows, D_MODEL), lambda l: (0, 0)),
            pl.BlockSpec((1, D_MODEL), lambda l: (0, 0)),
            pl.BlockSpec((1, D_MODEL, 2 * MEM_WIDTH), lambda l: (l, 0, 0)),
        ],
        out_specs=[
            pl.BlockSpec((1, batch, MEM_WIDTH, MEM_LEN), lambda l: (l, 0, 0, 0)),
            pl.BlockSpec((1, batch, MEM_LEN, MEM_WIDTH), lambda l: (l, 0, 0, 0)),
        ],
        out_shape=[
            jax.ShapeDtypeStruct((DEPTH, batch, MEM_WIDTH, MEM_LEN), BF16),
            jax.ShapeDtypeStruct((DEPTH, batch, MEM_LEN, MEM_WIDTH), BF16),
        ],
        compiler_params=_params(1),
        name="mem_kv",
    )(mem2d, mem_norm, w_kv)


def _mem_attn(qm, mkT_ref, mv_ref, fillers=()):
    fillers = list(fillers)

    def logits(hh):
        sl = slice(hh * HEAD_DIM, (hh + 1) * HEAD_DIM)
        return _dot(qm[:, sl].astype(BF16), mkT_ref[0, 0, sl, :]) * ATTN_SCALE

    outs = []
    s_next = logits(0)
    for hh in range(MEM_HEADS):
        s = s_next
        if hh + 1 < MEM_HEADS:
            s_next = logits(hh + 1)
        if fillers:
            fillers.pop(0)()
        p = jnp.exp(s - jnp.max(s, axis=-1, keepdims=True))
        denom = jnp.sum(p, axis=-1, keepdims=True)
        outs.append(_dot(p.astype(BF16), mv_ref[0, 0, :, hh * HEAD_DIM:(hh + 1) * HEAD_DIM]) / denom)
    for f in fillers:
        f()
    return jnp.concatenate(outs, axis=-1).astype(BF16)


def _ffn_kernel(x_ref, g_ref, wgu_ref, wd_ref, fg_ref, o_ref, act_ref, *, final):
    x = x_ref[...]
    h = _rmsnorm(x, g_ref[...]).astype(BF16)
    for c in range(D_FF // FF_CHUNK):
        lo = c * FF_CHUNK
        gate = _dot(h, wgu_ref[:, lo:lo + FF_CHUNK])
        up = _dot(h, wgu_ref[:, D_FF + lo:D_FF + lo + FF_CHUNK])
        act_ref[:, lo:lo + FF_CHUNK] = (gate * jax.nn.sigmoid(gate) * up).astype(BF16)
    y = x + 0.5 * _dot(act_ref[...], wd_ref[...])
    if final:
        y = _rmsnorm(y, fg_ref[...])
    o_ref[...] = y


def _ffn(x2d, g, wgu, wd, fg, final):
    tokens = x2d.shape[0]
    tm = TOKEN_TILE
    row = lambda i: (i, 0)
    return pl.pallas_call(
        functools.partial(_ffn_kernel, final=final),
        grid=(tokens // tm,),
        in_specs=[
            pl.BlockSpec((tm, D_MODEL), row),
            _resident((1, D_MODEL)),
            _resident((D_MODEL, 2 * D_FF)),
            _resident((D_FF, D_MODEL)),
            _resident((1, D_MODEL)),
        ],
        out_specs=pl.BlockSpec((tm, D_MODEL), row),
        out_shape=jax.ShapeDtypeStruct((tokens, D_MODEL), F32),
        scratch_shapes=[pltpu.VMEM((tm, D_FF), BF16)],
        compiler_params=_params(1),
        name="ffn_final" if final else "ffn",
    )(x2d, g, wgu, wd, fg)


def _mixer_a_kernel(x_ref, g_ref, win_ref, cw_ref, wout_ref, mkT_ref, mv_ref,
                    o_ref, tail_ref, *, tiles_per_seq):
    tm = x_ref.shape[0]
    c = CONV_DIM

    @pl.when(pl.program_id(0) % tiles_per_seq == 0)
    def _():
        tail_ref[...] = jnp.zeros_like(tail_ref)

    x = x_ref[...]
    h = _rmsnorm(x, g_ref[...]).astype(BF16)
    proj = {}

    def project(name, lo):
        def run():
            proj[name] = _dot(h, win_ref[:, lo:lo + c])
        return run

    y_mem = _mem_attn(_dot(h, win_ref[:, 3 * c:]), mkT_ref, mv_ref,
                      [project("gate_c", c), project("u", 2 * c), project("gate_b", 0)])
    v = proj["gate_c"] * proj["u"]

    prev = tail_ref[...]
    sub = lax.broadcasted_iota(jnp.int32, (V7X_SUBLANES, c), 0)
    shifted = []
    for k in range(1, CONV_WIDTH):
        rv = pltpu.roll(v, k, 0)
        head = jnp.where(sub < k, pltpu.roll(prev, k, 0), rv[:V7X_SUBLANES])
        shifted.append(jnp.concatenate([head, rv[V7X_SUBLANES:]], axis=0))
    tail_ref[...] = v[tm - V7X_SUBLANES:]

    cw = cw_ref[...]
    conv = cw[CONV_WIDTH - 1:CONV_WIDTH] * v
    for k in range(1, CONV_WIDTH):
        conv = conv + cw[CONV_WIDTH - 1 - k:CONV_WIDTH - k] * shifted[k - 1]
    y_tok = (proj["gate_b"] * conv).astype(BF16)
    o_ref[...] = x + _dot(y_tok, wout_ref[:c, :]) + _dot(y_mem, wout_ref[c:, :])


def _mixer_a(x2d, g, w_in, conv_w, w_out, mkT, mv, layer, seq):
    tokens = x2d.shape[0]
    tm = TOKEN_TILE
    tps = seq // tm
    row = lambda i: (i, 0)
    return pl.pallas_call(
        functools.partial(_mixer_a_kernel, tiles_per_seq=tps),
        grid=(tokens // tm,),
        in_specs=[
            pl.BlockSpec((tm, D_MODEL), row),
            _resident((1, D_MODEL)),
            _resident(w_in.shape),
            _resident(conv_w.shape),
            _resident(w_out.shape),
            pl.BlockSpec((1, 1, MEM_WIDTH, MEM_LEN), lambda i: (layer, i // tps, 0, 0)),
            pl.BlockSpec((1, 1, MEM_LEN, MEM_WIDTH), lambda i: (layer, i // tps, 0, 0)),
        ],
        out_specs=pl.BlockSpec((tm, D_MODEL), row),
        out_shape=jax.ShapeDtypeStruct((tokens, D_MODEL), F32),
        scratch_shapes=[pltpu.VMEM((V7X_SUBLANES, CONV_DIM), F32)],
        compiler_params=_params(1),
        name="mixer_a",
    )(x2d, g, w_in, conv_w, w_out, mkT, mv)


def _kvf_kernel(x_ref, g_ref, wkv_ref, wf_ref, bf_ref, tri_ref, sel_ref, hsum_ref,
                k_ref, bias_ref, vT_ref, c_ref, kn_ref, carry_ref, *, tiles_per_seq):
    tm = x_ref.shape[0]

    @pl.when(pl.program_id(0) % tiles_per_seq == 0)
    def _():
        carry_ref[...] = jnp.zeros_like(carry_ref)

    h = _rmsnorm(x_ref[...], g_ref[...]).astype(BF16)

    z = _dot(h, wf_ref[...]) + bf_ref[...]
    ls = -(jnp.maximum(-z, 0.0) + jnp.log1p(jnp.exp(-jnp.abs(z))))
    k = _dot(h, wkv_ref[:, :FOX_WIDTH])
    for hh in range(FOX_HEADS):
        k_ref[0, hh] = k[:, hh * HEAD_DIM:(hh + 1) * HEAD_DIM].astype(BF16)

    cs = _dot(tri_ref[...], jnp.concatenate(_split_bf16(ls), axis=1))
    cum = carry_ref[0:1, :]
    for p in range(N_SPLIT):
        cum = cum + cs[:, p * V7X_LANES:(p + 1) * V7X_LANES]
    carry_ref[...] = jnp.broadcast_to(cum[tm - 1:tm, :], carry_ref.shape)
    c_ref[0] = cum
    kn_ref[0] = jnp.broadcast_to(
        jnp.max(_dot((k * k).astype(BF16), hsum_ref[...]), axis=0, keepdims=True),
        kn_ref.shape[1:])
    v = _dot(h, wkv_ref[:, FOX_WIDTH:])

    bias = _dot(jnp.concatenate(_split_bf16(-LOG2E * cum), axis=1), sel_ref[...])
    lane = lax.broadcasted_iota(jnp.int32, (1, V7X_LANES), 1)
    ones_lanes = jnp.where((lane >= AUG_SHIFT) & (lane < AUG_SHIFT + N_SPLIT), 1.0, 0.0)
    bias_ref[0] = (bias + ones_lanes).astype(BF16)
    for hh in range(FOX_HEADS):
        sl = slice(hh * HEAD_DIM, (hh + 1) * HEAD_DIM)
        vT_ref[0, hh, :HEAD_DIM, :] = v[:, sl].T.astype(BF16)
    denom_rows = lax.broadcasted_iota(jnp.int32, (V_PAD_ROWS, tm), 0) == 0
    for hh in range(FOX_HEADS):
        vT_ref[0, hh, HEAD_DIM:, :] = jnp.where(denom_rows, 1.0, 0.0).astype(BF16)


def _kvf(x2d, g, w_kv, w_f, b_f, batch, seq):
    tokens = x2d.shape[0]
    tm = TOKEN_TILE
    tps = seq // tm
    tri = jnp.tril(jnp.ones((tm, tm), F32)).astype(BF16)
    sel = np.zeros((N_SPLIT * V7X_LANES, V7X_LANES), np.float32)
    for p in range(N_SPLIT):
        for hh in range(FOX_HEADS):
            sel[p * V7X_LANES + hh, N_SPLIT * hh + p] = 1.0
    sel = jnp.asarray(sel, BF16)
    hsum = _head_sum_matrix()
    return pl.pallas_call(
        functools.partial(_kvf_kernel, tiles_per_seq=tps),
        grid=(tokens // tm,),
        in_specs=[
            pl.BlockSpec((tm, D_MODEL), lambda i: (i, 0)),
            _resident((1, D_MODEL)),
            _resident(w_kv.shape),
            _resident(w_f.shape),
            _resident(b_f.shape),
            _resident(tri.shape),
            _resident(sel.shape),
            _resident(hsum.shape),
        ],
        out_specs=[
            pl.BlockSpec((1, FOX_HEADS, tm, HEAD_DIM), lambda i: (i // tps, 0, i % tps, 0)),
            pl.BlockSpec((1, tm, V7X_LANES), lambda i: (i // tps, i % tps, 0)),
            pl.BlockSpec((1, FOX_HEADS, V_ROWS, tm), lambda i: (i // tps, 0, 0, i % tps)),
            pl.BlockSpec((1, tm, V7X_LANES), lambda i: (i // tps, i % tps, 0)),
            pl.BlockSpec((1, V7X_SUBLANES, V7X_LANES), lambda i: (i, 0, 0)),
        ],
        out_shape=[
            jax.ShapeDtypeStruct((batch, FOX_HEADS, seq, HEAD_DIM), BF16),
            jax.ShapeDtypeStruct((batch, seq, V7X_LANES), BF16),
            jax.ShapeDtypeStruct((batch, FOX_HEADS, V_ROWS, seq), BF16),
            jax.ShapeDtypeStruct((batch, seq, V7X_LANES), F32),
            jax.ShapeDtypeStruct((tokens // tm, V7X_SUBLANES, V7X_LANES), F32),
        ],
        scratch_shapes=[pltpu.VMEM((V7X_SUBLANES, V7X_LANES), F32)],
        compiler_params=_params(1),
        name="kvf",
    )(x2d, g, w_kv, w_f, b_f, tri, sel, hsum)


def _qproj_kernel(x_ref, g_ref, wq_ref, mkT_ref, mv_ref, c_ref, kmax_ref,
                  qT_ref, ymem_ref, gap_ref):
    tm = x_ref.shape[0]
    h = _rmsnorm(x_ref[...], g_ref[...]).astype(BF16)
    cT = c_ref[0].T
    row = lax.broadcasted_iota(jnp.int32, (AUG_ROWS, tm), 0)
    gaps = []

    def finish_head(hh, q):
        qT = q.T
        reach =(BOUND_MARGIN * jnp.sqrt(jnp.sum(qT * qT, axis=0, keepdims=True))
                 * kmax_ref[0, hh:hh + 1, :])
        gaps.append(2.0 * reach)
        aug = jnp.where((row >= N_SPLIT * hh) & (row < N_SPLIT * (hh + 1)), 1.0, 0.0)
        for p, piece in enumerate(_split_bf16(LOG2E * cT[hh:hh + 1, :] - reach)):
            aug = jnp.where(row == AUG_SHIFT + p, piece.astype(F32), aug)
        qT_ref[0, hh, :HEAD_DIM, :] = qT.astype(BF16)
        qT_ref[0, hh, HEAD_DIM:HEAD_DIM + AUG_ROWS, :] = aug.astype(BF16)
        qT_ref[0, hh, HEAD_DIM + AUG_ROWS:, :] = jnp.zeros((HEAD_DIM - AUG_ROWS, tm), BF16)

    def project_pair(first):
        def run():
            q2 = (_dot(h, wq_ref[:, first * HEAD_DIM:(first + 2) * HEAD_DIM])
                  * (ATTN_SCALE * LOG2E))
            finish_head(first, q2[:, :HEAD_DIM])
            finish_head(first + 1, q2[:, HEAD_DIM:])
        return run

    ymem_ref[...] = _mem_attn(_dot(h, wq_ref[:, FOX_WIDTH:]), mkT_ref, mv_ref,
                              [project_pair(hh) for hh in range(0, FOX_HEADS, 2)])
    gap_ref[0] = jnp.broadcast_to(functools.reduce(jnp.maximum, gaps), gap_ref.shape[1:])


def _qproj(x2d, g, w_q, mkT, mv, c, kmax, layer, batch, seq):
    tokens = x2d.shape[0]
    tm = TOKEN_TILE
    tps = seq // tm
    return pl.pallas_call(
        _qproj_kernel,
        grid=(tokens // tm,),
        in_specs=[
            pl.BlockSpec((tm, D_MODEL), lambda i: (i, 0)),
            _resident((1, D_MODEL)),
            _resident(w_q.shape),
            pl.BlockSpec((1, 1, MEM_WIDTH, MEM_LEN), lambda i: (layer, i // tps, 0, 0)),
            pl.BlockSpec((1, 1, MEM_LEN, MEM_WIDTH), lambda i: (layer, i // tps, 0, 0)),
            pl.BlockSpec((1, tm, V7X_LANES), lambda i: (i // tps, i % tps, 0)),
            pl.BlockSpec((1, FOX_HEADS, tm), lambda i: (i // tps, 0, 0)),
        ],
        out_specs=[
            pl.BlockSpec((1, FOX_HEADS, 2 * HEAD_DIM, tm), lambda i: (i // tps, 0, 0, i % tps)),
            pl.BlockSpec((tm, MEM_WIDTH), lambda i: (i, 0)),
            pl.BlockSpec((1, V7X_SUBLANES, tm), lambda i: (i, 0, 0)),
        ],
        out_shape=[
            jax.ShapeDtypeStruct((batch, FOX_HEADS, 2 * HEAD_DIM, seq), BF16),
            jax.ShapeDtypeStruct((tokens, MEM_WIDTH), BF16),
            jax.ShapeDtypeStruct((tokens // tm, V7X_SUBLANES, tm), F32),
        ],
        compiler_params=_params(1),
        name="qproj",
    )(x2d, g, w_q, mkT, mv, c, kmax)


def _fox_kernel(qT_ref, k_ref, bias_ref, vT_ref, o_ref, m_ref, acc_ref, s_ref, *, online_max):
    t = ATTN_TILE
    tk = ATTN_KEY_TILE
    heads = qT_ref.shape[1]
    qi = pl.program_id(2)
    q0 = pl.multiple_of(qi * t, t)

    if online_max:
        m_ref[...] = jnp.full_like(m_ref, -jnp.inf)
    acc_ref[...] = jnp.zeros_like(acc_ref)

    def logits(unit):
        g, k0, _ = unit
        ka = jnp.concatenate([k_ref[0, g, pl.ds(k0, tk), :], bias_ref[0, pl.ds(k0, tk), :]],
                             axis=1)
        return _dot(ka, qT_ref[0, g])

    def softmax_step(unit, s):
        g, k0, diag_offset = unit
        if diag_offset is not None:
            causal = (lax.broadcasted_iota(jnp.int32, (tk, t), 0) + diag_offset
                      <= lax.broadcasted_iota(jnp.int32, (tk, t), 1))
            s = jnp.where(causal, s, NEG_INF)
        vt = vT_ref[0, g, :, pl.ds(k0, tk)]
        if online_max:
            m_prev = m_ref[g]
            m_new = jnp.maximum(m_prev, jnp.max(s, axis=0, keepdims=True))
            alpha = jnp.exp2(m_prev - m_new)
            acc_ref[g] = alpha * acc_ref[g] + _dot(vt, jnp.exp2(s - m_new).astype(BF16))
            m_ref[g] = m_new
        else:
            acc_ref[g] += _dot(vt, jnp.exp2(s).astype(BF16))

    def first_units(k0):
        return [(g, k0, None) for g in range(ATTN_LOOKAHEAD)]

    def run(units, following):
        pending = [s_ref[j] for j in range(ATTN_LOOKAHEAD)]
        seq = units + following
        for i, u in enumerate(units):
            s = pending.pop(0)
            if i + ATTN_LOOKAHEAD < len(seq):
                pending.append(logits(seq[i + ATTN_LOOKAHEAD]))
            softmax_step(u, s)
        for j, s in enumerate(pending):
            s_ref[j] = s

    def body(ki, c):
        base = ki * (ATTN_KEY_STEPS * tk)
        run([(g, pl.multiple_of(base + j * tk, tk), None)
             for j in range(ATTN_KEY_STEPS) for g in range(heads)],
            first_units(pl.multiple_of(base + ATTN_KEY_STEPS * tk, tk)))
        return c

    for j, u in enumerate(first_units(0)):
        s_ref[j] = logits(u)
    lax.fori_loop(0, qi * (t // (ATTN_KEY_STEPS * tk)), body, 0)
    run([(g, pl.multiple_of(q0 + d * tk, tk), d * tk)
         for d in range(t // tk) for g in range(heads)], [])
    for g in range(heads):
        out = acc_ref[g, :HEAD_DIM, :] / acc_ref[g, HEAD_DIM:HEAD_DIM + 1, :]
        o_ref[0, :, g * HEAD_DIM:(g + 1) * HEAD_DIM] = out.T.astype(BF16)


def _fox_attention(qT, k, bias, vT, online_max):
    batch, heads, _, seq = qT.shape
    t = ATTN_TILE
    hg = ATTN_HEADS
    assert ATTN_LOOKAHEAD <= hg and t % (ATTN_KEY_STEPS * ATTN_KEY_TILE) == 0
    once = pl.Buffered(1)
    return pl.pallas_call(
        functools.partial(_fox_kernel, online_max=online_max),
        grid=(batch, heads // hg, seq // t),
        in_specs=[
            pl.BlockSpec((1, hg, 2 * HEAD_DIM, t), lambda b, h, q: (b, h, 0, q)),
            pl.BlockSpec((1, hg, seq, HEAD_DIM), lambda b, h, q: (b, h, 0, 0), pipeline_mode=once),
            pl.BlockSpec((1, seq, V7X_LANES), lambda b, h, q: (b, 0, 0), pipeline_mode=once),
            pl.BlockSpec((1, hg, V_ROWS, seq), lambda b, h, q: (b, h, 0, 0), pipeline_mode=once),
        ],
        out_specs=pl.BlockSpec((1, t, hg * HEAD_DIM), lambda b, h, q: (b, q, h)),
        out_shape=jax.ShapeDtypeStruct((batch, seq, heads * HEAD_DIM), BF16),
        scratch_shapes=[
            pltpu.VMEM((hg, 1, t), F32),
            pltpu.VMEM((hg, V_ROWS, t), F32),
            pltpu.VMEM((ATTN_LOOKAHEAD, ATTN_KEY_TILE, t), F32),
        ],
        compiler_params=_params(3),
        name="fox_attention_online" if online_max else "fox_attention_bounded",
    )(qT, k, bias, vT)


def _outproj_kernel(x_ref, yt_ref, ym_ref, w_ref, o_ref):
    o_ref[...] = (x_ref[...] + _dot(yt_ref[...], w_ref[:FOX_WIDTH, :])
                  + _dot(ym_ref[...], w_ref[FOX_WIDTH:, :]))


def _outproj(x2d, y_tok, y_mem, w_out):
    tokens = x2d.shape[0]
    tm = TOKEN_TILE
    row = lambda i: (i, 0)
    return pl.pallas_call(
        _outproj_kernel,
        grid=(tokens // tm,),
        in_specs=[
            pl.BlockSpec((tm, D_MODEL), row),
            pl.BlockSpec((tm, FOX_WIDTH), row),
            pl.BlockSpec((tm, MEM_WIDTH), row),
            _resident(w_out.shape),
        ],
        out_specs=pl.BlockSpec((tm, D_MODEL), row),
        out_shape=jax.ShapeDtypeStruct((tokens, D_MODEL), F32),
        compiler_params=_params(1),
        name="outproj",
    )(x2d, y_tok, y_mem, w_out)


def kernel(x, mem, ffn_norm, ffn_w_gate_up, ffn_w_down, mix_norm, mem_norm, mem_w_kv,
           a_w_in, a_conv_w, a_w_out, kv_norm, w_kvf, b_f, b_w_q, b_w_out, final_norm):
    batch, seq, d = x.shape
    assert d == D_MODEL and seq % TOKEN_TILE == 0 and seq % ATTN_TILE == 0
    assert mem.shape == (batch, MEM_LEN, D_MODEL)
    tokens = batch * seq

    wgu = ffn_w_gate_up.astype(BF16)
    wd = ffn_w_down.astype(BF16)
    w_mem = mem_w_kv.astype(BF16)
    w_in = a_w_in.astype(BF16)
    w_out_a = a_w_out.astype(BF16)
    w_kv = w_kvf[:, :2 * FOX_WIDTH].astype(BF16)
    w_f = jnp.pad(w_kvf[:, 2 * FOX_WIDTH:], ((0, 0), (0, V7X_LANES - FOX_HEADS))).astype(BF16)
    b_f_row = jnp.pad(b_f, (0, V7X_LANES - FOX_HEADS)).reshape(1, V7X_LANES)
    w_q = b_w_q.astype(BF16)
    w_out_b = b_w_out.astype(BF16)
    fg = final_norm.reshape(1, D_MODEL)

    mkT, mv = _mem_kv(mem.reshape(batch * MEM_LEN, D_MODEL), mem_norm.reshape(1, D_MODEL),
                      w_mem, batch)

    h = x.reshape(tokens, D_MODEL)
    k_sh = bias_sh = vT = c_sh = k_max = None
    for l in range(DEPTH):
        if l == N_A:
            k_sh, bias_sh, vT, c_sh, kn = _kvf(h, kv_norm.reshape(1, D_MODEL), w_kv, w_f, b_f_row,
                                               batch, seq)
            k_max = jnp.sqrt(jnp.max(kn[:, 0, :FOX_HEADS].reshape(batch, -1, FOX_HEADS), axis=1))
            k_max = jnp.broadcast_to(k_max[:, :, None], (batch, FOX_HEADS, TOKEN_TILE))
        h = _ffn(h, ffn_norm[l, 0].reshape(1, D_MODEL), wgu[l, 0], wd[l, 0], fg, False)
        g_mix = mix_norm[l].reshape(1, D_MODEL)
        if l < N_A:
            h = _mixer_a(h, g_mix, w_in[l], a_conv_w[l], w_out_a[l], mkT, mv, l, seq)
        else:
            j = l - N_A
            qT, y_mem, gap = _qproj(h, g_mix, w_q[j], mkT, mv, c_sh, k_max, l, batch, seq)
            y_tok = lax.cond(
                jnp.max(gap) < MAX_SHIFT_GAP,
                functools.partial(_fox_attention, online_max=False),
                functools.partial(_fox_attention, online_max=True),
                qT, k_sh, bias_sh, vT).reshape(tokens, FOX_WIDTH)
            h = _outproj(h, y_tok, y_mem, w_out_b[j])
        h = _ffn(h, ffn_norm[l, 1].reshape(1, D_MODEL), wgu[l, 1], wd[l, 1], fg, l == DEPTH - 1)
    return h.reshape(batch, seq, D_MODEL)
```

```python
import functools
import math

import jax
import jax.numpy as jnp
import numpy as np
from jax import lax
from jax.experimental import pallas as pl
from jax.experimental.pallas import tpu as pltpu

D_MODEL = 1024
DEPTH = 4
N_A = DEPTH // 2
MEM_LEN = 256
MEM_HEADS = 4
HEAD_DIM = 128
MEM_WIDTH = MEM_HEADS * HEAD_DIM
CONV_DIM = D_MODEL
CONV_WIDTH = 3
FOX_HEADS = 8
FOX_WIDTH = FOX_HEADS * HEAD_DIM
D_FF = 2816
RMS_EPS = 1e-6
NEG_INF = float(np.finfo(np.float32).min)
ATTN_SCALE = HEAD_DIM ** -0.5
LOG2E = math.log2(math.e)

V7X_LANES = 128
V7X_SUBLANES = 8
V7X_MXU_DIM = 256
V7X_VMEM_LIMIT_BYTES = 60000 * 1024

TOKEN_TILE = 512
FFN_TOKEN_TILE = 1024
FF_CHUNK = V7X_MXU_DIM
ATTN_TILE = 512
ATTN_KEY_TILE = 256
ATTN_KEY_STEPS = 2
ATTN_LOOKAHEAD = 2
ATTN_HEADS = 8
N_SPLIT = 3
V_PAD_ROWS = 16
V_ROWS = HEAD_DIM + V_PAD_ROWS
AUG_SHIFT = N_SPLIT * FOX_HEADS
AUG_ROWS = 32
assert AUG_SHIFT + N_SPLIT <= AUG_ROWS <= HEAD_DIM
BOUND_MARGIN = 1.02
MAX_SHIFT_GAP = 100.0

F32 = jnp.float32
BF16 = jnp.bfloat16


def _dot(a, b):
    return jnp.dot(a, b, preferred_element_type=F32)


def _rmsnorm(x, g):
    y = x * lax.rsqrt(jnp.mean(x * x, axis=-1, keepdims=True) + RMS_EPS)
    return y * g


def _split_bf16(x):
    pieces = []
    r = x
    for _ in range(N_SPLIT):
        p = r.astype(BF16)
        pieces.append(p)
        r = r - p.astype(F32)
    return pieces


def _head_sum_matrix():
    m = np.zeros((FOX_WIDTH, V7X_LANES), np.float32)
    for hh in range(FOX_HEADS):
        m[hh * HEAD_DIM:(hh + 1) * HEAD_DIM, hh] = 1.0
    return jnp.asarray(m, BF16)


def _params(n_grid_dims):
    return pltpu.CompilerParams(
        dimension_semantics=("arbitrary",) * n_grid_dims,
        vmem_limit_bytes=V7X_VMEM_LIMIT_BYTES)


def _resident(shape):
    zeros = (0,) * len(shape)
    return pl.BlockSpec(shape, lambda *_: zeros, pipeline_mode=pl.Buffered(1))


def _mem_kv_kernel(mem_ref, g_ref, w_ref, mkT_ref, mv_ref, *, batch):
    h = _rmsnorm(mem_ref[...], g_ref[...]).astype(BF16)
    mkv = _dot(h, w_ref[0])
    for b in range(batch):
        blk = mkv[b * MEM_LEN:(b + 1) * MEM_LEN]
        mkT_ref[0, b] = blk[:, :MEM_WIDTH].T.astype(BF16)
        mv_ref[0, b] = blk[:, MEM_WIDTH:].astype(BF16)


def _mem_kv(mem2d, mem_norm, w_kv, batch):
    rows = mem2d.shape[0]
    return pl.pallas_call(
        functools.partial(_mem_kv_kernel, batch=batch),
        grid=(DEPTH,),
        in_specs=[
            pl.BlockSpec((rows, D_MODEL), lambda l: (0, 0)),
            pl.BlockSpec((1, D_MODEL), lambda l: (0, 0)),
            pl.BlockSpec((1, D_MODEL, 2 * MEM_WIDTH), lambda l: (l, 0, 0)),
        ],
        out_specs=[
            pl.BlockSpec((1, batch, MEM_WIDTH, MEM_LEN), lambda l: (l, 0, 0, 0)),
            pl.BlockSpec((1, batch, MEM_LEN, MEM_WIDTH), lambda l: (l, 0, 0, 0)),
        ],
        out_shape=[
            jax.ShapeDtypeStruct((DEPTH, batch, MEM_WIDTH, MEM_LEN), BF16),
            jax.ShapeDtypeStruct((DEPTH, batch, MEM_LEN, MEM_WIDTH), BF16),
        ],
        compiler_params=_params(1),
        name="mem_kv",
    )(mem2d, mem_norm, w_kv)


def _mem_attn(qm, mkT_ref, mv_ref, fillers=()):
    fillers = list(fillers)

    def logits(hh):
        sl = slice(hh * HEAD_DIM, (hh + 1) * HEAD_DIM)
        return _dot(qm[:, sl].astype(BF16), mkT_ref[0, 0, sl, :]) * ATTN_SCALE

    outs = []
    s_next = logits(0)
    for hh in range(MEM_HEADS):
        s = s_next
        if hh + 1 < MEM_HEADS:
            s_next = logits(hh + 1)
        if fillers:
            fillers.pop(0)()
        p = jnp.exp(s - jnp.max(s, axis=-1, keepdims=True))
        denom = jnp.sum(p, axis=-1, keepdims=True)
        outs.append(_dot(p.astype(BF16), mv_ref[0, 0, :, hh * HEAD_DIM:(hh + 1) * HEAD_DIM]) / denom)
    for f in fillers:
        f()
    return jnp.concatenate(outs, axis=-1).astype(BF16)


def _ffn_kernel(*refs, final, mixed):
    if mixed:
        x_ref, yt_ref, ym_ref, wout_ref, g_ref, wgu_ref, wd_ref, fg_ref, o_ref, act_ref = refs
        x = (x_ref[...] + _dot(yt_ref[...], wout_ref[:FOX_WIDTH, :])
             + _dot(ym_ref[...], wout_ref[FOX_WIDTH:, :]))
    else:
        x_ref, g_ref, wgu_ref, wd_ref, fg_ref, o_ref, act_ref = refs
        x = x_ref[...]
    h = _rmsnorm(x, g_ref[...]).astype(BF16)
    for c in range(D_FF // FF_CHUNK):
        lo = c * FF_CHUNK
        gate = _dot(h, wgu_ref[:, lo:lo + FF_CHUNK])
        up = _dot(h, wgu_ref[:, D_FF + lo:D_FF + lo + FF_CHUNK])
        act_ref[:, lo:lo + FF_CHUNK] = (gate * jax.nn.sigmoid(gate) * up).astype(BF16)
    y = x + 0.5 * _dot(act_ref[...], wd_ref[...])
    if final:
        y = _rmsnorm(y, fg_ref[...])
    o_ref[...] = y


def _ffn(x2d, g, wgu, wd, fg, final, mix=None):
    tokens = x2d.shape[0]
    tm = TOKEN_TILE if mix else FFN_TOKEN_TILE
    row = lambda i: (i, 0)
    weights = [_resident((1, D_MODEL)), _resident((D_MODEL, 2 * D_FF)),
               _resident((D_FF, D_MODEL)), _resident((1, D_MODEL))]
    if mix:
        y_tok, y_mem, w_out = mix
        operands = (x2d, y_tok, y_mem, w_out, g, wgu, wd, fg)
        in_specs = [pl.BlockSpec((tm, D_MODEL), row), pl.BlockSpec((tm, FOX_WIDTH), row),
                    pl.BlockSpec((tm, MEM_WIDTH), row), _resident(w_out.shape)] + weights
    else:
        operands = (x2d, g, wgu, wd, fg)
        in_specs = [pl.BlockSpec((tm, D_MODEL), row)] + weights
    return pl.pallas_call(
        functools.partial(_ffn_kernel, final=final, mixed=bool(mix)),
        grid=(tokens // tm,),
        in_specs=in_specs,
        out_specs=pl.BlockSpec((tm, D_MODEL), row),
        out_shape=jax.ShapeDtypeStruct((tokens, D_MODEL), F32),
        scratch_shapes=[pltpu.VMEM((tm, D_FF), BF16)],
        compiler_params=_params(1),
        name=("ffn_mixed" if mix else "ffn") + ("_final" if final else ""),
    )(*operands)


def _mixer_a_kernel(x_ref, g_ref, win_ref, cw_ref, wout_ref, mkT_ref, mv_ref,
                    o_ref, tail_ref, *, tiles_per_seq):
    tm = x_ref.shape[0]
    c = CONV_DIM

    @pl.when(pl.program_id(0) % tiles_per_seq == 0)
    def _():
        tail_ref[...] = jnp.zeros_like(tail_ref)

    x = x_ref[...]
    h = _rmsnorm(x, g_ref[...]).astype(BF16)
    proj = {}

    def project(name, lo):
        def run():
            proj[name] = _dot(h, win_ref[:, lo:lo + c])
        return run

    y_mem = _mem_attn(_dot(h, win_ref[:, 3 * c:]), mkT_ref, mv_ref,
                      [project("gate_c", c), project("u", 2 * c), project("gate_b", 0)])
    v = proj["gate_c"] * proj["u"]

    prev = tail_ref[...]
    sub = lax.broadcasted_iota(jnp.int32, (V7X_SUBLANES, c), 0)
    shifted = []
    for k in range(1, CONV_WIDTH):
        rv = pltpu.roll(v, k, 0)
        head = jnp.where(sub < k, pltpu.roll(prev, k, 0), rv[:V7X_SUBLANES])
        shifted.append(jnp.concatenate([head, rv[V7X_SUBLANES:]], axis=0))
    tail_ref[...] = v[tm - V7X_SUBLANES:]

    cw = cw_ref[...]
    conv = cw[CONV_WIDTH - 1:CONV_WIDTH] * v
    for k in range(1, CONV_WIDTH):
        conv = conv + cw[CONV_WIDTH - 1 - k:CONV_WIDTH - k] * shifted[k - 1]
    y_tok = (proj["gate_b"] * conv).astype(BF16)
    o_ref[...] = x + _dot(y_tok, wout_ref[:c, :]) + _dot(y_mem, wout_ref[c:, :])


def _mixer_a(x2d, g, w_in, conv_w, w_out, mkT, mv, layer, seq):
    tokens = x2d.shape[0]
    tm = TOKEN_TILE
    tps = seq // tm
    row = lambda i: (i, 0)
    return pl.pallas_call(
        functools.partial(_mixer_a_kernel, tiles_per_seq=tps),
        grid=(tokens // tm,),
        in_specs=[
            pl.BlockSpec((tm, D_MODEL), row),
            _resident((1, D_MODEL)),
            _resident(w_in.shape),
            _resident(conv_w.shape),
            _resident(w_out.shape),
            pl.BlockSpec((1, 1, MEM_WIDTH, MEM_LEN), lambda i: (layer, i // tps, 0, 0)),
            pl.BlockSpec((1, 1, MEM_LEN, MEM_WIDTH), lambda i: (layer, i // tps, 0, 0)),
        ],
        out_specs=pl.BlockSpec((tm, D_MODEL), row),
        out_shape=jax.ShapeDtypeStruct((tokens, D_MODEL), F32),
        scratch_shapes=[pltpu.VMEM((V7X_SUBLANES, CONV_DIM), F32)],
        compiler_params=_params(1),
        name="mixer_a",
    )(x2d, g, w_in, conv_w, w_out, mkT, mv)


def _kvf_kernel(x_ref, g_ref, wkv_ref, wf_ref, bf_ref, tri_ref, sel_ref, hsum_ref,
                k_ref, bias_ref, vT_ref, c_ref, kn_ref, carry_ref, *, tiles_per_seq):
    tm = x_ref.shape[0]

    @pl.when(pl.program_id(0) % tiles_per_seq == 0)
    def _():
        carry_ref[...] = jnp.zeros_like(carry_ref)

    h = _rmsnorm(x_ref[...], g_ref[...]).astype(BF16)

    z = _dot(h, wf_ref[...]) + bf_ref[...]
    ls = -(jnp.maximum(-z, 0.0) + jnp.log1p(jnp.exp(-jnp.abs(z))))
    k = _dot(h, wkv_ref[:, :FOX_WIDTH])
    for hh in range(FOX_HEADS):
        k_ref[0, hh] = k[:, hh * HEAD_DIM:(hh + 1) * HEAD_DIM].astype(BF16)

    cs = _dot(tri_ref[...], jnp.concatenate(_split_bf16(ls), axis=1))
    cum = carry_ref[0:1, :]
    for p in range(N_SPLIT):
        cum = cum + cs[:, p * V7X_LANES:(p + 1) * V7X_LANES]
    carry_ref[...] = jnp.broadcast_to(cum[tm - 1:tm, :], carry_ref.shape)
    c_ref[0] = cum
    kn_ref[0] = jnp.broadcast_to(
        jnp.max(_dot((k * k).astype(BF16), hsum_ref[...]), axis=0, keepdims=True),
        kn_ref.shape[1:])
    v = _dot(h, wkv_ref[:, FOX_WIDTH:])

    bias = _dot(jnp.concatenate(_split_bf16(-LOG2E * cum), axis=1), sel_ref[...])
    lane = lax.broadcasted_iota(jnp.int32, (1, V7X_LANES), 1)
    ones_lanes = jnp.where((lane >= AUG_SHIFT) & (lane < AUG_SHIFT + N_SPLIT), 1.0, 0.0)
    bias_ref[0] = (bias + ones_lanes).astype(BF16)
    for hh in range(FOX_HEADS):
        sl = slice(hh * HEAD_DIM, (hh + 1) * HEAD_DIM)
        vT_ref[0, hh, :HEAD_DIM, :] = v[:, sl].T.astype(BF16)
    denom_rows = lax.broadcasted_iota(jnp.int32, (V_PAD_ROWS, tm), 0) == 0
    for hh in range(FOX_HEADS):
        vT_ref[0, hh, HEAD_DIM:, :] = jnp.where(denom_rows, 1.0, 0.0).astype(BF16)


def _kvf(x2d, g, w_kv, w_f, b_f, batch, seq):
    tokens = x2d.shape[0]
    tm = TOKEN_TILE
    tps = seq // tm
    tri = jnp.tril(jnp.ones((tm, tm), F32)).astype(BF16)
    sel = np.zeros((N_SPLIT * V7X_LANES, V7X_LANES), np.float32)
    for p in range(N_SPLIT):
        for hh in range(FOX_HEADS):
            sel[p * V7X_LANES + hh, N_SPLIT * hh + p] = 1.0
    sel = jnp.asarray(sel, BF16)
    hsum = _head_sum_matrix()
    return pl.pallas_call(
        functools.partial(_kvf_kernel, tiles_per_seq=tps),
        grid=(tokens // tm,),
        in_specs=[
            pl.BlockSpec((tm, D_MODEL), lambda i: (i, 0)),
            _resident((1, D_MODEL)),
            _resident(w_kv.shape),
            _resident(w_f.shape),
            _resident(b_f.shape),
            _resident(tri.shape),
            _resident(sel.shape),
            _resident(hsum.shape),
        ],
        out_specs=[
            pl.BlockSpec((1, FOX_HEADS, tm, HEAD_DIM), lambda i: (i // tps, 0, i % tps, 0)),
            pl.BlockSpec((1, tm, V7X_LANES), lambda i: (i // tps, i % tps, 0)),
            pl.BlockSpec((1, FOX_HEADS, V_ROWS, tm), lambda i: (i // tps, 0, 0, i % tps)),
            pl.BlockSpec((1, tm, V7X_LANES), lambda i: (i // tps, i % tps, 0)),
            pl.BlockSpec((1, V7X_SUBLANES, V7X_LANES), lambda i: (i, 0, 0)),
        ],
        out_shape=[
            jax.ShapeDtypeStruct((batch, FOX_HEADS, seq, HEAD_DIM), BF16),
            jax.ShapeDtypeStruct((batch, seq, V7X_LANES), BF16),
            jax.ShapeDtypeStruct((batch, FOX_HEADS, V_ROWS, seq), BF16),
            jax.ShapeDtypeStruct((batch, seq, V7X_LANES), F32),
            jax.ShapeDtypeStruct((tokens // tm, V7X_SUBLANES, V7X_LANES), F32),
        ],
        scratch_shapes=[pltpu.VMEM((V7X_SUBLANES, V7X_LANES), F32)],
        compiler_params=_params(1),
        name="kvf",
    )(x2d, g, w_kv, w_f, b_f, tri, sel, hsum)


def _qproj_kernel(x_ref, g_ref, wq_ref, mkT_ref, mv_ref, c_ref, kmax_ref,
                  qT_ref, ymem_ref, gap_ref):
    tm = x_ref.shape[0]
    h = _rmsnorm(x_ref[...], g_ref[...]).astype(BF16)
    cT = c_ref[0].T
    row = lax.broadcasted_iota(jnp.int32, (AUG_ROWS, tm), 0)
    gaps = []

    def finish_head(hh, q):
        qT = q.T
        reach =(BOUND_MARGIN * jnp.sqrt(jnp.sum(qT * qT, axis=0, keepdims=True))
                 * kmax_ref[0, hh:hh + 1, :])
        gaps.append(2.0 * reach)
        aug = jnp.where((row >= N_SPLIT * hh) & (row < N_SPLIT * (hh + 1)), 1.0, 0.0)
        for p, piece in enumerate(_split_bf16(LOG2E * cT[hh:hh + 1, :] - reach)):
            aug = jnp.where(row == AUG_SHIFT + p, piece.astype(F32), aug)
        qT_ref[0, hh, :HEAD_DIM, :] = qT.astype(BF16)
        qT_ref[0, hh, HEAD_DIM:HEAD_DIM + AUG_ROWS, :] = aug.astype(BF16)
        qT_ref[0, hh, HEAD_DIM + AUG_ROWS:, :] = jnp.zeros((HEAD_DIM - AUG_ROWS, tm), BF16)

    def project_pair(first):
        def run():
            q2 = (_dot(h, wq_ref[:, first * HEAD_DIM:(first + 2) * HEAD_DIM])
                  * (ATTN_SCALE * LOG2E))
            finish_head(first, q2[:, :HEAD_DIM])
            finish_head(first + 1, q2[:, HEAD_DIM:])
        return run

    ymem_ref[...] = _mem_attn(_dot(h, wq_ref[:, FOX_WIDTH:]), mkT_ref, mv_ref,
                              [project_pair(hh) for hh in range(0, FOX_HEADS, 2)])
    gap_ref[0] = jnp.broadcast_to(functools.reduce(jnp.maximum, gaps), gap_ref.shape[1:])


def _qproj(x2d, g, w_q, mkT, mv, c, kmax, layer, batch, seq):
    tokens = x2d.shape[0]
    tm = TOKEN_TILE
    tps = seq // tm
    return pl.pallas_call(
        _qproj_kernel,
        grid=(tokens // tm,),
        in_specs=[
            pl.BlockSpec((tm, D_MODEL), lambda i: (i, 0)),
            _resident((1, D_MODEL)),
            _resident(w_q.shape),
            pl.BlockSpec((1, 1, MEM_WIDTH, MEM_LEN), lambda i: (layer, i // tps, 0, 0)),
            pl.BlockSpec((1, 1, MEM_LEN, MEM_WIDTH), lambda i: (layer, i // tps, 0, 0)),
            pl.BlockSpec((1, tm, V7X_LANES), lambda i: (i // tps, i % tps, 0)),
            pl.BlockSpec((1, FOX_HEADS, tm), lambda i: (i // tps, 0, 0)),
        ],
        out_specs=[
            pl.BlockSpec((1, FOX_HEADS, 2 * HEAD_DIM, tm), lambda i: (i // tps, 0, 0, i % tps)),
            pl.BlockSpec((tm, MEM_WIDTH), lambda i: (i, 0)),
            pl.BlockSpec((1, V7X_SUBLANES, tm), lambda i: (i, 0, 0)),
        ],
        out_shape=[
            jax.ShapeDtypeStruct((batch, FOX_HEADS, 2 * HEAD_DIM, seq), BF16),
            jax.ShapeDtypeStruct((tokens, MEM_WIDTH), BF16),
            jax.ShapeDtypeStruct((tokens // tm, V7X_SUBLANES, tm), F32),
        ],
        compiler_params=_params(1),
        name="qproj",
    )(x2d, g, w_q, mkT, mv, c, kmax)


def _fox_kernel(qT_ref, k_ref, bias_ref, vT_ref, o_ref, m_ref, acc_ref, s_ref, *, online_max):
    t = ATTN_TILE
    tk = ATTN_KEY_TILE
    heads = qT_ref.shape[1]
    qi = pl.program_id(2)
    q0 = pl.multiple_of(qi * t, t)

    if online_max:
        m_ref[...] = jnp.full_like(m_ref, -jnp.inf)
    acc_ref[...] = jnp.zeros_like(acc_ref)

    def logits(unit):
        g, k0, _ = unit
        ka = jnp.concatenate([k_ref[0, g, pl.ds(k0, tk), :], bias_ref[0, pl.ds(k0, tk), :]],
                             axis=1)
        return _dot(ka, qT_ref[0, g])

    def softmax_step(unit, s):
        g, k0, diag_offset = unit
        if diag_offset is not None:
            causal = (lax.broadcasted_iota(jnp.int32, (tk, t), 0) + diag_offset
                      <= lax.broadcasted_iota(jnp.int32, (tk, t), 1))
            s = jnp.where(causal, s, NEG_INF)
        vt = vT_ref[0, g, :, pl.ds(k0, tk)]
        if online_max:
            m_prev = m_ref[g]
            m_new = jnp.maximum(m_prev, jnp.max(s, axis=0, keepdims=True))
            alpha = jnp.exp2(m_prev - m_new)
            acc_ref[g] = alpha * acc_ref[g] + _dot(vt, jnp.exp2(s - m_new).astype(BF16))
            m_ref[g] = m_new
        else:
            acc_ref[g] += _dot(vt, jnp.exp2(s).astype(BF16))

    def first_units(k0):
        return [(g, k0, None) for g in range(ATTN_LOOKAHEAD)]

    def run(units, following):
        pending = [s_ref[j] for j in range(ATTN_LOOKAHEAD)]
        seq = units + following
        for i, u in enumerate(units):
            s = pending.pop(0)
            if i + ATTN_LOOKAHEAD < len(seq):
                pending.append(logits(seq[i + ATTN_LOOKAHEAD]))
            softmax_step(u, s)
        for j, s in enumerate(pending):
            s_ref[j] = s

    def body(ki, c):
        base = ki * (ATTN_KEY_STEPS * tk)
        run([(g, pl.multiple_of(base + j * tk, tk), None)
             for j in range(ATTN_KEY_STEPS) for g in range(heads)],
            first_units(pl.multiple_of(base + ATTN_KEY_STEPS * tk, tk)))
        return c

    for j, u in enumerate(first_units(0)):
        s_ref[j] = logits(u)
    lax.fori_loop(0, qi * (t // (ATTN_KEY_STEPS * tk)), body, 0)
    run([(g, pl.multiple_of(q0 + d * tk, tk), d * tk)
         for d in range(t // tk) for g in range(heads)], [])
    for g in range(heads):
        out = acc_ref[g, :HEAD_DIM, :] / acc_ref[g, HEAD_DIM:HEAD_DIM + 1, :]
        o_ref[0, :, g * HEAD_DIM:(g + 1) * HEAD_DIM] = out.T.astype(BF16)


def _fox_attention(qT, k, bias, vT, online_max):
    batch, heads, _, seq = qT.shape
    t = ATTN_TILE
    hg = ATTN_HEADS
    assert ATTN_LOOKAHEAD <= hg and t % (ATTN_KEY_STEPS * ATTN_KEY_TILE) == 0
    once = pl.Buffered(1)
    return pl.pallas_call(
        functools.partial(_fox_kernel, online_max=online_max),
        grid=(batch, heads // hg, seq // t),
        in_specs=[
            pl.BlockSpec((1, hg, 2 * HEAD_DIM, t), lambda b, h, q: (b, h, 0, q)),
            pl.BlockSpec((1, hg, seq, HEAD_DIM), lambda b, h, q: (b, h, 0, 0), pipeline_mode=once),
            pl.BlockSpec((1, seq, V7X_LANES), lambda b, h, q: (b, 0, 0), pipeline_mode=once),
            pl.BlockSpec((1, hg, V_ROWS, seq), lambda b, h, q: (b, h, 0, 0), pipeline_mode=once),
        ],
        out_specs=pl.BlockSpec((1, t, hg * HEAD_DIM), lambda b, h, q: (b, q, h)),
        out_shape=jax.ShapeDtypeStruct((batch, seq, heads * HEAD_DIM), BF16),
        scratch_shapes=[
            pltpu.VMEM((hg, 1, t), F32),
            pltpu.VMEM((hg, V_ROWS, t), F32),
            pltpu.VMEM((ATTN_LOOKAHEAD, ATTN_KEY_TILE, t), F32),
        ],
        compiler_params=_params(3),
        name="fox_attention_online" if online_max else "fox_attention_bounded",
    )(qT, k, bias, vT)


def kernel(x, mem, ffn_norm, ffn_w_gate_up, ffn_w_down, mix_norm, mem_norm, mem_w_kv,
           a_w_in, a_conv_w, a_w_out, kv_norm, w_kvf, b_f, b_w_q, b_w_out, final_norm):
    batch, seq, d = x.shape
    assert d == D_MODEL and seq % TOKEN_TILE == 0 and seq % ATTN_TILE == 0
    assert mem.shape == (batch, MEM_LEN, D_MODEL)
    tokens = batch * seq

    wgu = ffn_w_gate_up.astype(BF16)
    wd = ffn_w_down.astype(BF16)
    w_mem = mem_w_kv.astype(BF16)
    w_in = a_w_in.astype(BF16)
    w_out_a = a_w_out.astype(BF16)
    w_kv = w_kvf[:, :2 * FOX_WIDTH].astype(BF16)
    w_f = jnp.pad(w_kvf[:, 2 * FOX_WIDTH:], ((0, 0), (0, V7X_LANES - FOX_HEADS))).astype(BF16)
    b_f_row = jnp.pad(b_f, (0, V7X_LANES - FOX_HEADS)).reshape(1, V7X_LANES)
    w_q = b_w_q.astype(BF16)
    w_out_b = b_w_out.astype(BF16)
    fg = final_norm.reshape(1, D_MODEL)

    mkT, mv = _mem_kv(mem.reshape(batch * MEM_LEN, D_MODEL), mem_norm.reshape(1, D_MODEL),
                      w_mem, batch)

    h = x.reshape(tokens, D_MODEL)
    k_sh = bias_sh = vT = c_sh = k_max = None
    for l in range(DEPTH):
        if l == N_A:
            k_sh, bias_sh, vT, c_sh, kn = _kvf(h, kv_norm.reshape(1, D_MODEL), w_kv, w_f, b_f_row,
                                               batch, seq)
            k_max = jnp.sqrt(jnp.max(kn[:, 0, :FOX_HEADS].reshape(batch, -1, FOX_HEADS), axis=1))
            k_max = jnp.broadcast_to(k_max[:, :, None], (batch, FOX_HEADS, TOKEN_TILE))
        h = _ffn(h, ffn_norm[l, 0].reshape(1, D_MODEL), wgu[l, 0], wd[l, 0], fg, False)
        g_mix = mix_norm[l].reshape(1, D_MODEL)
        mix = None
        if l < N_A:
            h = _mixer_a(h, g_mix, w_in[l], a_conv_w[l], w_out_a[l], mkT, mv, l, seq)
        else:
            j = l - N_A
            qT, y_mem, gap = _qproj(h, g_mix, w_q[j], mkT, mv, c_sh, k_max, l, batch, seq)
            y_tok = lax.cond(
                jnp.max(gap) < MAX_SHIFT_GAP,
                functools.partial(_fox_attention, online_max=False),
                functools.partial(_fox_attention, online_max=True),
                qT, k_sh, bias_sh, vT).reshape(tokens, FOX_WIDTH)
            mix = (y_tok, y_mem, w_out_b[j])
        h = _ffn(h, ffn_norm[l, 1].reshape(1, D_MODEL), wgu[l, 1], wd[l, 1], fg, l == DEPTH - 1, mix)
    return h.reshape(batch, seq, D_MODEL)
```

```python
import functools
import math

import jax
import jax.numpy as jnp
import numpy as np
from jax import lax
from jax.experimental import pallas as pl
from jax.experimental.pallas import tpu as pltpu

D_MODEL = 1024
DEPTH = 4
N_A = DEPTH // 2
MEM_LEN = 256
MEM_HEADS = 4
HEAD_DIM = 128
MEM_WIDTH = MEM_HEADS * HEAD_DIM
CONV_DIM = D_MODEL
CONV_WIDTH = 3
FOX_HEADS = 8
FOX_WIDTH = FOX_HEADS * HEAD_DIM
D_FF = 2816
RMS_EPS = 1e-6
NEG_INF = float(np.finfo(np.float32).min)
ATTN_SCALE = HEAD_DIM ** -0.5
LOG2E = math.log2(math.e)

V7X_LANES = 128
V7X_SUBLANES = 8
V7X_MXU_DIM = 256
V7X_VMEM_LIMIT_BYTES = 60000 * 1024

TOKEN_TILE = 512
FFN_TOKEN_TILE = 1024
FF_CHUNK = V7X_MXU_DIM
ATTN_TILE = 512
ATTN_KEY_TILE = 256
ATTN_LOOKAHEAD = 2
ATTN_HEADS = 8
N_SPLIT = 3
V_PAD_ROWS = 16
V_ROWS = HEAD_DIM + V_PAD_ROWS
AUG_SHIFT = N_SPLIT * FOX_HEADS
AUG_ROWS = 32
assert AUG_SHIFT + N_SPLIT <= AUG_ROWS <= HEAD_DIM
BOUND_MARGIN = 1.02
MAX_SHIFT_GAP = 100.0

F32 = jnp.float32
BF16 = jnp.bfloat16


def _dot(a, b):
    return jnp.dot(a, b, preferred_element_type=F32)


def _rmsnorm(x, g):
    y = x * lax.rsqrt(jnp.mean(x * x, axis=-1, keepdims=True) + RMS_EPS)
    return y * g


def _split_bf16(x):
    pieces = []
    r = x
    for _ in range(N_SPLIT):
        p = r.astype(BF16)
        pieces.append(p)
        r = r - p.astype(F32)
    return pieces


def _head_sum_matrix():
    m = np.zeros((FOX_WIDTH, V7X_LANES), np.float32)
    for hh in range(FOX_HEADS):
        m[hh * HEAD_DIM:(hh + 1) * HEAD_DIM, hh] = 1.0
    return jnp.asarray(m, BF16)


def _params(n_grid_dims):
    return pltpu.CompilerParams(
        dimension_semantics=("arbitrary",) * n_grid_dims,
        vmem_limit_bytes=V7X_VMEM_LIMIT_BYTES)


def _resident(shape):
    zeros = (0,) * len(shape)
    return pl.BlockSpec(shape, lambda *_: zeros, pipeline_mode=pl.Buffered(1))


def _resident_layer(stacked, index):
    n = len(index)
    block_index = tuple(index) + (0,) * (stacked.ndim - n)
    return pl.BlockSpec((None,) * n + tuple(stacked.shape[n:]), lambda *_: block_index,
                        pipeline_mode=pl.Buffered(1))


def _mem_kv_kernel(mem_ref, g_ref, w_ref, mkT_ref, mv_ref, *, batch):
    h = _rmsnorm(mem_ref[...], g_ref[...]).astype(BF16)
    mkv = _dot(h, w_ref[0])
    for b in range(batch):
        blk = mkv[b * MEM_LEN:(b + 1) * MEM_LEN]
        mkT_ref[0, b] = blk[:, :MEM_WIDTH].T.astype(BF16)
        mv_ref[0, b] = blk[:, MEM_WIDTH:].astype(BF16)


def _mem_kv(mem2d, mem_norm, w_kv, batch):
    rows = mem2d.shape[0]
    return pl.pallas_call(
        functools.partial(_mem_kv_kernel, batch=batch),
        grid=(DEPTH,),
        in_specs=[
            pl.BlockSpec((rows, D_MODEL), lambda l: (0, 0)),
            pl.BlockSpec((1, D_MODEL), lambda l: (0, 0)),
            pl.BlockSpec((1, D_MODEL, 2 * MEM_WIDTH), lambda l: (l, 0, 0)),
        ],
        out_specs=[
            pl.BlockSpec((1, batch, MEM_WIDTH, MEM_LEN), lambda l: (l, 0, 0, 0)),
            pl.BlockSpec((1, batch, MEM_LEN, MEM_WIDTH), lambda l: (l, 0, 0, 0)),
        ],
        out_shape=[
            jax.ShapeDtypeStruct((DEPTH, batch, MEM_WIDTH, MEM_LEN), BF16),
            jax.ShapeDtypeStruct((DEPTH, batch, MEM_LEN, MEM_WIDTH), BF16),
        ],
        compiler_params=_params(1),
        name="mem_kv",
    )(mem2d, mem_norm, w_kv)


def _mem_attn(qm, mkT_ref, mv_ref, fillers=()):
    fillers = list(fillers)

    def logits(hh):
        sl = slice(hh * HEAD_DIM, (hh + 1) * HEAD_DIM)
        return _dot(qm[:, sl].astype(BF16), mkT_ref[0, 0, sl, :]) * ATTN_SCALE

    outs = []
    s_next = logits(0)
    for hh in range(MEM_HEADS):
        s = s_next
        if hh + 1 < MEM_HEADS:
            s_next = logits(hh + 1)
        if fillers:
            fillers.pop(0)()
        p = jnp.exp(s - jnp.max(s, axis=-1, keepdims=True))
        denom = jnp.sum(p, axis=-1, keepdims=True)
        outs.append(_dot(p.astype(BF16), mv_ref[0, 0, :, hh * HEAD_DIM:(hh + 1) * HEAD_DIM]) / denom)
    for f in fillers:
        f()
    return jnp.concatenate(outs, axis=-1).astype(BF16)


def _ffn_kernel(*refs, final, mixed):
    if mixed:
        x_ref, yt_ref, ym_ref, wout_ref, g_ref, wgu_ref, wd_ref, fg_ref, o_ref, act_ref = refs
        x = (x_ref[...] + _dot(yt_ref[...], wout_ref[:FOX_WIDTH, :])
             + _dot(ym_ref[...], wout_ref[FOX_WIDTH:, :]))
    else:
        x_ref, g_ref, wgu_ref, wd_ref, fg_ref, o_ref, act_ref = refs
        x = x_ref[...]
    h = _rmsnorm(x, g_ref[...]).astype(BF16)
    for c in range(D_FF // FF_CHUNK):
        lo = c * FF_CHUNK
        gate = _dot(h, wgu_ref[:, lo:lo + FF_CHUNK])
        up = _dot(h, wgu_ref[:, D_FF + lo:D_FF + lo + FF_CHUNK])
        act_ref[:, lo:lo + FF_CHUNK] = (gate * jax.nn.sigmoid(gate) * up).astype(BF16)
    y = x + 0.5 * _dot(act_ref[...], wd_ref[...])
    if final:
        y = _rmsnorm(y, fg_ref[...])
    o_ref[...] = y


def _ffn(x2d, g, wgu, wd, which, fg, final, mix=None):
    tokens = x2d.shape[0]
    tm = TOKEN_TILE if mix else FFN_TOKEN_TILE
    row = lambda i: (i, 0)
    weights = [_resident((1, D_MODEL)), _resident_layer(wgu, which),
               _resident_layer(wd, which), _resident((1, D_MODEL))]
    if mix:
        y_tok, y_mem, w_out, j = mix
        operands = (x2d, y_tok, y_mem, w_out, g, wgu, wd, fg)
        in_specs = [pl.BlockSpec((tm, D_MODEL), row), pl.BlockSpec((tm, FOX_WIDTH), row),
                    pl.BlockSpec((tm, MEM_WIDTH), row), _resident_layer(w_out, (j,))] + weights
    else:
        operands = (x2d, g, wgu, wd, fg)
        in_specs = [pl.BlockSpec((tm, D_MODEL), row)] + weights
    return pl.pallas_call(
        functools.partial(_ffn_kernel, final=final, mixed=bool(mix)),
        grid=(tokens // tm,),
        in_specs=in_specs,
        out_specs=pl.BlockSpec((tm, D_MODEL), row),
        out_shape=jax.ShapeDtypeStruct((tokens, D_MODEL), F32),
        scratch_shapes=[pltpu.VMEM((tm, D_FF), BF16)],
        compiler_params=_params(1),
        name=("ffn_mixed" if mix else "ffn") + ("_final" if final else ""),
    )(*operands)


def _mixer_a_kernel(x_ref, g_ref, win_ref, cw_ref, wout_ref, mkT_ref, mv_ref,
                    o_ref, tail_ref, *, tiles_per_seq):
    tm = x_ref.shape[0]
    c = CONV_DIM

    @pl.when(pl.program_id(0) % tiles_per_seq == 0)
    def _():
        tail_ref[...] = jnp.zeros_like(tail_ref)

    x = x_ref[...]
    h = _rmsnorm(x, g_ref[...]).astype(BF16)
    proj = {}

    def project(name, lo):
        def run():
            proj[name] = _dot(h, win_ref[:, lo:lo + c])
        return run

    y_mem = _mem_attn(_dot(h, win_ref[:, 3 * c:]), mkT_ref, mv_ref,
                      [project("gate_c", c), project("u", 2 * c), project("gate_b", 0)])
    v = proj["gate_c"] * proj["u"]

    prev = tail_ref[...]
    sub = lax.broadcasted_iota(jnp.int32, (V7X_SUBLANES, c), 0)
    shifted = []
    for k in range(1, CONV_WIDTH):
        rv = pltpu.roll(v, k, 0)
        head = jnp.where(sub < k, pltpu.roll(prev, k, 0), rv[:V7X_SUBLANES])
        shifted.append(jnp.concatenate([head, rv[V7X_SUBLANES:]], axis=0))
    tail_ref[...] = v[tm - V7X_SUBLANES:]

    cw = cw_ref[...]
    conv = cw[CONV_WIDTH - 1:CONV_WIDTH] * v
    for k in range(1, CONV_WIDTH):
        conv = conv + cw[CONV_WIDTH - 1 - k:CONV_WIDTH - k] * shifted[k - 1]
    y_tok = (proj["gate_b"] * conv).astype(BF16)
    o_ref[...] = x + _dot(y_tok, wout_ref[:c, :]) + _dot(y_mem, wout_ref[c:, :])


def _mixer_a(x2d, g, w_in, conv_w, w_out, mkT, mv, layer, seq):
    tokens = x2d.shape[0]
    tm = TOKEN_TILE
    tps = seq // tm
    row = lambda i: (i, 0)
    return pl.pallas_call(
        functools.partial(_mixer_a_kernel, tiles_per_seq=tps),
        grid=(tokens // tm,),
        in_specs=[
            pl.BlockSpec((tm, D_MODEL), row),
            _resident((1, D_MODEL)),
            _resident_layer(w_in, (layer,)),
            _resident_layer(conv_w, (layer,)),
            _resident_layer(w_out, (layer,)),
            pl.BlockSpec((1, 1, MEM_WIDTH, MEM_LEN), lambda i: (layer, i // tps, 0, 0)),
            pl.BlockSpec((1, 1, MEM_LEN, MEM_WIDTH), lambda i: (layer, i // tps, 0, 0)),
        ],
        out_specs=pl.BlockSpec((tm, D_MODEL), row),
        out_shape=jax.ShapeDtypeStruct((tokens, D_MODEL), F32),
        scratch_shapes=[pltpu.VMEM((V7X_SUBLANES, CONV_DIM), F32)],
        compiler_params=_params(1),
        name="mixer_a",
    )(x2d, g, w_in, conv_w, w_out, mkT, mv)


def _kvf_kernel(x_ref, g_ref, wkv_ref, wf_ref, bf_ref, tri_ref, sel_ref, hsum_ref,
                k_ref, bias_ref, vT_ref, c_ref, kn_ref, carry_ref, *, tiles_per_seq):
    tm = x_ref.shape[0]

    @pl.when(pl.program_id(0) % tiles_per_seq == 0)
    def _():
        carry_ref[...] = jnp.zeros_like(carry_ref)

    h = _rmsnorm(x_ref[...], g_ref[...]).astype(BF16)

    z = _dot(h, wf_ref[...]) + bf_ref[...]
    ls = -(jnp.maximum(-z, 0.0) + jnp.log1p(jnp.exp(-jnp.abs(z))))
    k = _dot(h, wkv_ref[:, :FOX_WIDTH])
    for hh in range(FOX_HEADS):
        k_ref[0, hh] = k[:, hh * HEAD_DIM:(hh + 1) * HEAD_DIM].astype(BF16)

    cs = _dot(tri_ref[...], jnp.concatenate(_split_bf16(ls), axis=1))
    cum = carry_ref[0:1, :]
    for p in range(N_SPLIT):
        cum = cum + cs[:, p * V7X_LANES:(p + 1) * V7X_LANES]
    carry_ref[...] = jnp.broadcast_to(cum[tm - 1:tm, :], carry_ref.shape)
    c_ref[0] = cum
    kn_ref[0] = jnp.broadcast_to(
        jnp.max(_dot((k * k).astype(BF16), hsum_ref[...]), axis=0, keepdims=True),
        kn_ref.shape[1:])
    v = _dot(h, wkv_ref[:, FOX_WIDTH:])

    bias = _dot(jnp.concatenate(_split_bf16(-LOG2E * cum), axis=1), sel_ref[...])
    lane = lax.broadcasted_iota(jnp.int32, (1, V7X_LANES), 1)
    ones_lanes = jnp.where((lane >= AUG_SHIFT) & (lane < AUG_SHIFT + N_SPLIT), 1.0, 0.0)
    bias_ref[0] = (bias + ones_lanes).astype(BF16)
    for hh in range(FOX_HEADS):
        sl = slice(hh * HEAD_DIM, (hh + 1) * HEAD_DIM)
        vT_ref[0, hh, :HEAD_DIM, :] = v[:, sl].T.astype(BF16)
    denom_rows = lax.broadcasted_iota(jnp.int32, (V_PAD_ROWS, tm), 0) == 0
    for hh in range(FOX_HEADS):
        vT_ref[0, hh, HEAD_DIM:, :] = jnp.where(denom_rows, 1.0, 0.0).astype(BF16)


def _kvf(x2d, g, w_kv, w_f, b_f, batch, seq):
    tokens = x2d.shape[0]
    tm = TOKEN_TILE
    tps = seq // tm
    tri = jnp.tril(jnp.ones((tm, tm), F32)).astype(BF16)
    sel = np.zeros((N_SPLIT * V7X_LANES, V7X_LANES), np.float32)
    for p in range(N_SPLIT):
        for hh in range(FOX_HEADS):
            sel[p * V7X_LANES + hh, N_SPLIT * hh + p] = 1.0
    sel = jnp.asarray(sel, BF16)
    hsum = _head_sum_matrix()
    return pl.pallas_call(
        functools.partial(_kvf_kernel, tiles_per_seq=tps),
        grid=(tokens // tm,),
        in_specs=[
            pl.BlockSpec((tm, D_MODEL), lambda i: (i, 0)),
            _resident((1, D_MODEL)),
            _resident(w_kv.shape),
            _resident(w_f.shape),
            _resident(b_f.shape),
            _resident(tri.shape),
            _resident(sel.shape),
            _resident(hsum.shape),
        ],
        out_specs=[
            pl.BlockSpec((1, FOX_HEADS, tm, HEAD_DIM), lambda i: (i // tps, 0, i % tps, 0)),
            pl.BlockSpec((1, tm, V7X_LANES), lambda i: (i // tps, i % tps, 0)),
            pl.BlockSpec((1, FOX_HEADS, V_ROWS, tm), lambda i: (i // tps, 0, 0, i % tps)),
            pl.BlockSpec((1, tm, V7X_LANES), lambda i: (i // tps, i % tps, 0)),
            pl.BlockSpec((1, V7X_SUBLANES, V7X_LANES), lambda i: (i, 0, 0)),
        ],
        out_shape=[
            jax.ShapeDtypeStruct((batch, FOX_HEADS, seq, HEAD_DIM), BF16),
            jax.ShapeDtypeStruct((batch, seq, V7X_LANES), BF16),
            jax.ShapeDtypeStruct((batch, FOX_HEADS, V_ROWS, seq), BF16),
            jax.ShapeDtypeStruct((batch, seq, V7X_LANES), F32),
            jax.ShapeDtypeStruct((tokens // tm, V7X_SUBLANES, V7X_LANES), F32),
        ],
        scratch_shapes=[pltpu.VMEM((V7X_SUBLANES, V7X_LANES), F32)],
        compiler_params=_params(1),
        name="kvf",
    )(x2d, g, w_kv, w_f, b_f, tri, sel, hsum)


def _qproj_kernel(x_ref, g_ref, wq_ref, mkT_ref, mv_ref, c_ref, kmax_ref,
                  qT_ref, ymem_ref, gap_ref):
    tm = x_ref.shape[0]
    h = _rmsnorm(x_ref[...], g_ref[...]).astype(BF16)
    cT = c_ref[0].T
    row = lax.broadcasted_iota(jnp.int32, (AUG_ROWS, tm), 0)
    gaps = []

    def finish_head(hh, q):
        qT = q.T
        reach =(BOUND_MARGIN * jnp.sqrt(jnp.sum(qT * qT, axis=0, keepdims=True))
                 * kmax_ref[0, hh:hh + 1, :])
        gaps.append(2.0 * reach)
        aug = jnp.where((row >= N_SPLIT * hh) & (row < N_SPLIT * (hh + 1)), 1.0, 0.0)
        for p, piece in enumerate(_split_bf16(LOG2E * cT[hh:hh + 1, :] - reach)):
            aug = jnp.where(row == AUG_SHIFT + p, piece.astype(F32), aug)
        qT_ref[0, hh, :HEAD_DIM, :] = qT.astype(BF16)
        qT_ref[0, hh, HEAD_DIM:HEAD_DIM + AUG_ROWS, :] = aug.astype(BF16)
        qT_ref[0, hh, HEAD_DIM + AUG_ROWS:, :] = jnp.zeros((HEAD_DIM - AUG_ROWS, tm), BF16)

    def project_pair(first):
        def run():
            q2 = (_dot(h, wq_ref[:, first * HEAD_DIM:(first + 2) * HEAD_DIM])
                  * (ATTN_SCALE * LOG2E))
            finish_head(first, q2[:, :HEAD_DIM])
            finish_head(first + 1, q2[:, HEAD_DIM:])
        return run

    ymem_ref[...] = _mem_attn(_dot(h, wq_ref[:, FOX_WIDTH:]), mkT_ref, mv_ref,
                              [project_pair(hh) for hh in range(0, FOX_HEADS, 2)])
    gap_ref[0] = jnp.broadcast_to(functools.reduce(jnp.maximum, gaps), gap_ref.shape[1:])


def _qproj(x2d, g, w_q, mkT, mv, c, kmax, layer, batch, seq):
    tokens = x2d.shape[0]
    tm = TOKEN_TILE
    tps = seq // tm
    return pl.pallas_call(
        _qproj_kernel,
        grid=(tokens // tm,),
        in_specs=[
            pl.BlockSpec((tm, D_MODEL), lambda i: (i, 0)),
            _resident((1, D_MODEL)),
            _resident_layer(w_q, (layer - N_A,)),
            pl.BlockSpec((1, 1, MEM_WIDTH, MEM_LEN), lambda i: (layer, i // tps, 0, 0)),
            pl.BlockSpec((1, 1, MEM_LEN, MEM_WIDTH), lambda i: (layer, i // tps, 0, 0)),
            pl.BlockSpec((1, tm, V7X_LANES), lambda i: (i // tps, i % tps, 0)),
            pl.BlockSpec((1, FOX_HEADS, tm), lambda i: (i // tps, 0, 0)),
        ],
        out_specs=[
            pl.BlockSpec((1, FOX_HEADS, 2 * HEAD_DIM, tm), lambda i: (i // tps, 0, 0, i % tps)),
            pl.BlockSpec((tm, MEM_WIDTH), lambda i: (i, 0)),
            pl.BlockSpec((1, V7X_SUBLANES, tm), lambda i: (i, 0, 0)),
        ],
        out_shape=[
            jax.ShapeDtypeStruct((batch, FOX_HEADS, 2 * HEAD_DIM, seq), BF16),
            jax.ShapeDtypeStruct((tokens, MEM_WIDTH), BF16),
            jax.ShapeDtypeStruct((tokens // tm, V7X_SUBLANES, tm), F32),
        ],
        compiler_params=_params(1),
        name="qproj",
    )(x2d, g, w_q, mkT, mv, c, kmax)


def _fox_kernel(qT_ref, k_ref, bias_ref, vT_ref, o_ref, m_ref, acc_ref, s_ref, *, online_max):
    t = ATTN_TILE
    tk = ATTN_KEY_TILE
    heads = qT_ref.shape[1]
    qi = pl.program_id(2)
    q0 = pl.multiple_of(qi * t, t)

    if online_max:
        m_ref[...] = jnp.full_like(m_ref, -jnp.inf)
    acc_ref[...] = jnp.zeros_like(acc_ref)

    def logits(unit):
        g, k0, diag_offset = unit
        ka = jnp.concatenate([k_ref[0, g, pl.ds(k0, tk), :], bias_ref[0, pl.ds(k0, tk), :]],
                             axis=1)
        return _dot(ka, qT_ref[0, g, :, (diag_offset or 0):])

    def softmax_step(unit, s):
        g, k0, diag_offset = unit
        lo = diag_offset or 0
        if diag_offset is not None:
            causal = (lax.broadcasted_iota(jnp.int32, s.shape, 0)
                      <= lax.broadcasted_iota(jnp.int32, s.shape, 1))
            s = jnp.where(causal, s, NEG_INF)
        vt = vT_ref[0, g, :, pl.ds(k0, tk)]
        if online_max:
            m_prev = m_ref[g, :, lo:]
            m_new = jnp.maximum(m_prev, jnp.max(s, axis=0, keepdims=True))
            alpha = jnp.exp2(m_prev - m_new)
            acc_ref[g, :, lo:] = (alpha * acc_ref[g, :, lo:]
                                  + _dot(vt, jnp.exp2(s - m_new).astype(BF16)))
            m_ref[g, :, lo:] = m_new
        else:
            acc_ref[g, :, lo:] += _dot(vt, jnp.exp2(s).astype(BF16))

    def first_units(k0):
        return [(g, k0, None) for g in range(ATTN_LOOKAHEAD)]

    def run(units, following):
        pending = [s_ref[j] for j in range(ATTN_LOOKAHEAD)]
        seq = units + following
        for i, u in enumerate(units):
            s = pending.pop(0)
            if i + ATTN_LOOKAHEAD < len(seq):
                pending.append(logits(seq[i + ATTN_LOOKAHEAD]))
            softmax_step(u, s)
        for j, s in enumerate(pending):
            s_ref[j] = s

    def full_keys(base, steps):
        run([(g, pl.multiple_of(base + j * tk, tk), None)
             for j in range(steps) for g in range(heads)],
            first_units(pl.multiple_of(base + steps * tk, tk)))

    tile_steps = t // tk

    def body(ki, c):
        full_keys(ki * (2 * t), 2 * tile_steps)
        return c

    for j, u in enumerate(first_units(0)):
        s_ref[j] = logits(u)
    lax.fori_loop(0, qi // 2, body, 0)

    @pl.when(qi % 2 == 1)
    def _():
        full_keys((qi - 1) * t, tile_steps)

    run([(g, pl.multiple_of(q0 + d * tk, tk), d * tk)
         for d in range(t // tk) for g in range(heads)], [])
    for g in range(heads):
        out = acc_ref[g, :HEAD_DIM, :] / acc_ref[g, HEAD_DIM:HEAD_DIM + 1, :]
        o_ref[0, :, g * HEAD_DIM:(g + 1) * HEAD_DIM] = out.T.astype(BF16)


def _fox_attention(qT, k, bias, vT, online_max):
    batch, heads, _, seq = qT.shape
    t = ATTN_TILE
    hg = ATTN_HEADS
    assert ATTN_LOOKAHEAD <= hg and t % ATTN_KEY_TILE == 0
    once = pl.Buffered(1)
    return pl.pallas_call(
        functools.partial(_fox_kernel, online_max=online_max),
        grid=(batch, heads // hg, seq // t),
        in_specs=[
            pl.BlockSpec((1, hg, 2 * HEAD_DIM, t), lambda b, h, q: (b, h, 0, q)),
            pl.BlockSpec((1, hg, seq, HEAD_DIM), lambda b, h, q: (b, h, 0, 0), pipeline_mode=once),
            pl.BlockSpec((1, seq, V7X_LANES), lambda b, h, q: (b, 0, 0), pipeline_mode=once),
            pl.BlockSpec((1, hg, V_ROWS, seq), lambda b, h, q: (b, h, 0, 0), pipeline_mode=once),
        ],
        out_specs=pl.BlockSpec((1, t, hg * HEAD_DIM), lambda b, h, q: (b, q, h)),
        out_shape=jax.ShapeDtypeStruct((batch, seq, heads * HEAD_DIM), BF16),
        scratch_shapes=[
            pltpu.VMEM((hg, 1, t), F32),
            pltpu.VMEM((hg, V_ROWS, t), F32),
            pltpu.VMEM((ATTN_LOOKAHEAD, ATTN_KEY_TILE, t), F32),
        ],
        compiler_params=_params(3),
        name="fox_attention_online" if online_max else "fox_attention_bounded",
    )(qT, k, bias, vT)


def kernel(x, mem, ffn_norm, ffn_w_gate_up, ffn_w_down, mix_norm, mem_norm, mem_w_kv,
           a_w_in, a_conv_w, a_w_out, kv_norm, w_kvf, b_f, b_w_q, b_w_out, final_norm):
    batch, seq, d = x.shape
    assert d == D_MODEL and seq % TOKEN_TILE == 0 and seq % ATTN_TILE == 0
    assert mem.shape == (batch, MEM_LEN, D_MODEL)
    tokens = batch * seq

    wgu = ffn_w_gate_up.astype(BF16)
    wd = ffn_w_down.astype(BF16)
    w_mem = mem_w_kv.astype(BF16)
    w_in = a_w_in.astype(BF16)
    w_out_a = a_w_out.astype(BF16)
    w_kv = w_kvf[:, :2 * FOX_WIDTH].astype(BF16)
    w_f = jnp.pad(w_kvf[:, 2 * FOX_WIDTH:], ((0, 0), (0, V7X_LANES - FOX_HEADS))).astype(BF16)
    b_f_row = jnp.pad(b_f, (0, V7X_LANES - FOX_HEADS)).reshape(1, V7X_LANES)
    w_q = b_w_q.astype(BF16)
    w_out_b = b_w_out.astype(BF16)
    fg = final_norm.reshape(1, D_MODEL)

    mkT, mv = _mem_kv(mem.reshape(batch * MEM_LEN, D_MODEL), mem_norm.reshape(1, D_MODEL),
                      w_mem, batch)

    h = x.reshape(tokens, D_MODEL)
    k_sh = bias_sh = vT = c_sh = k_max = None
    for l in range(DEPTH):
        if l == N_A:
            k_sh, bias_sh, vT, c_sh, kn = _kvf(h, kv_norm.reshape(1, D_MODEL), w_kv, w_f, b_f_row,
                                               batch, seq)
            k_max = jnp.sqrt(jnp.max(kn[:, 0, :FOX_HEADS].reshape(batch, -1, FOX_HEADS), axis=1))
            k_max = jnp.broadcast_to(k_max[:, :, None], (batch, FOX_HEADS, TOKEN_TILE))
        h = _ffn(h, ffn_norm[l, 0].reshape(1, D_MODEL), wgu, wd, (l, 0), fg, False)
        g_mix = mix_norm[l].reshape(1, D_MODEL)
        mix = None
        if l < N_A:
            h = _mixer_a(h, g_mix, w_in, a_conv_w, w_out_a, mkT, mv, l, seq)
        else:
            j = l - N_A
            qT, y_mem, gap = _qproj(h, g_mix, w_q, mkT, mv, c_sh, k_max, l, batch, seq)
            y_tok = lax.cond(
                jnp.max(gap) < MAX_SHIFT_GAP,
                functools.partial(_fox_attention, online_max=False),
                functools.partial(_fox_attention, online_max=True),
                qT, k_sh, bias_sh, vT).reshape(tokens, FOX_WIDTH)
            mix = (y_tok, y_mem, w_out_b, j)
        h = _ffn(h, ffn_norm[l, 1].reshape(1, D_MODEL), wgu, wd, (l, 1), fg, l == DEPTH - 1, mix)
    return h.reshape(batch, seq, D_MODEL)
```

```python
import functools
import math

import jax
import jax.numpy as jnp
import numpy as np
from jax import lax
from jax.experimental import pallas as pl
from jax.experimental.pallas import tpu as pltpu

D_MODEL = 1024
DEPTH = 4
N_A = DEPTH // 2
MEM_LEN = 256
MEM_HEADS = 4
HEAD_DIM = 128
MEM_WIDTH = MEM_HEADS * HEAD_DIM
CONV_DIM = D_MODEL
CONV_WIDTH = 3
FOX_HEADS = 8
FOX_WIDTH = FOX_HEADS * HEAD_DIM
D_FF = 2816
RMS_EPS = 1e-6
NEG_INF = float(np.finfo(np.float32).min)
ATTN_SCALE = HEAD_DIM ** -0.5
LOG2E = math.log2(math.e)

V7X_LANES = 128
V7X_SUBLANES = 8
V7X_MXU_DIM = 256
V7X_VMEM_LIMIT_BYTES = 60000 * 1024

TOKEN_TILE = 512
FFN_TOKEN_TILE = 1024
FF_CHUNK = V7X_MXU_DIM
ATTN_TILE = 512
ATTN_KEY_TILE = 256
ATTN_LOOKAHEAD = 2
ATTN_HEADS = 8
N_SPLIT = 3
V_PAD_ROWS = 16
V_ROWS = HEAD_DIM + V_PAD_ROWS
AUG_SHIFT = N_SPLIT * FOX_HEADS
AUG_ROWS = 32
assert AUG_SHIFT + N_SPLIT <= AUG_ROWS <= HEAD_DIM
BOUND_MARGIN = 1.02
MAX_SHIFT_GAP = 100.0

F32 = jnp.float32
BF16 = jnp.bfloat16


def _dot(a, b):
    return jnp.dot(a, b, preferred_element_type=F32)


def _rmsnorm(x, g):
    y = x * lax.rsqrt(jnp.mean(x * x, axis=-1, keepdims=True) + RMS_EPS)
    return y * g


def _split_bf16(x):
    pieces = []
    r = x
    for _ in range(N_SPLIT):
        p = r.astype(BF16)
        pieces.append(p)
        r = r - p.astype(F32)
    return pieces


def _head_sum_matrix():
    m = np.zeros((FOX_WIDTH, V7X_LANES), np.float32)
    for hh in range(FOX_HEADS):
        m[hh * HEAD_DIM:(hh + 1) * HEAD_DIM, hh] = 1.0
    return jnp.asarray(m, BF16)


def _params(n_grid_dims):
    return pltpu.CompilerParams(
        dimension_semantics=("arbitrary",) * n_grid_dims,
        vmem_limit_bytes=V7X_VMEM_LIMIT_BYTES)


def _resident(shape):
    zeros = (0,) * len(shape)
    return pl.BlockSpec(shape, lambda *_: zeros, pipeline_mode=pl.Buffered(1))


def _resident_layer(stacked, index):
    n = len(index)
    block_index = tuple(index) + (0,) * (stacked.ndim - n)
    return pl.BlockSpec((None,) * n + tuple(stacked.shape[n:]), lambda *_: block_index,
                        pipeline_mode=pl.Buffered(1))


def _mem_kv_kernel(mem_ref, g_ref, w_ref, mkT_ref, mv_ref, *, batch):
    h = _rmsnorm(mem_ref[...], g_ref[...]).astype(BF16)
    mkv = _dot(h, w_ref[0])
    for b in range(batch):
        blk = mkv[b * MEM_LEN:(b + 1) * MEM_LEN]
        mkT_ref[0, b] = blk[:, :MEM_WIDTH].T.astype(BF16)
        mv_ref[0, b] = blk[:, MEM_WIDTH:].astype(BF16)


def _mem_kv(mem2d, mem_norm, w_kv, batch):
    rows = mem2d.shape[0]
    return pl.pallas_call(
        functools.partial(_mem_kv_kernel, batch=batch),
        grid=(DEPTH,),
        in_specs=[
            pl.BlockSpec((rows, D_MODEL), lambda l: (0, 0)),
            pl.BlockSpec((1, D_MODEL), lambda l: (0, 0)),
            pl.BlockSpec((1, D_MODEL, 2 * MEM_WIDTH), lambda l: (l, 0, 0)),
        ],
        out_specs=[
            pl.BlockSpec((1, batch, MEM_WIDTH, MEM_LEN), lambda l: (l, 0, 0, 0)),
            pl.BlockSpec((1, batch, MEM_LEN, MEM_WIDTH), lambda l: (l, 0, 0, 0)),
        ],
        out_shape=[
            jax.ShapeDtypeStruct((DEPTH, batch, MEM_WIDTH, MEM_LEN), BF16),
            jax.ShapeDtypeStruct((DEPTH, batch, MEM_LEN, MEM_WIDTH), BF16),
        ],
        compiler_params=_params(1),
        name="mem_kv",
    )(mem2d, mem_norm, w_kv)


def _mem_attn(qm, mkT_ref, mv_ref, fillers=()):
    fillers = list(fillers)

    def logits(hh):
        sl = slice(hh * HEAD_DIM, (hh + 1) * HEAD_DIM)
        return _dot(qm[:, sl].astype(BF16), mkT_ref[0, 0, sl, :]) * ATTN_SCALE

    outs = []
    s_next = logits(0)
    for hh in range(MEM_HEADS):
        s = s_next
        if hh + 1 < MEM_HEADS:
            s_next = logits(hh + 1)
        if fillers:
            fillers.pop(0)()
        p = jnp.exp(s - jnp.max(s, axis=-1, keepdims=True))
        denom = jnp.sum(p, axis=-1, keepdims=True)
        outs.append(_dot(p.astype(BF16), mv_ref[0, 0, :, hh * HEAD_DIM:(hh + 1) * HEAD_DIM]) / denom)
    for f in fillers:
        f()
    return jnp.concatenate(outs, axis=-1).astype(BF16)


def _ffn_kernel(*refs, final, mixed):
    if mixed:
        x_ref, yt_ref, ym_ref, wout_ref, g_ref, wgu_ref, wd_ref, fg_ref, o_ref, act_ref = refs
        x = (x_ref[...] + _dot(yt_ref[...], wout_ref[:FOX_WIDTH, :])
             + _dot(ym_ref[...], wout_ref[FOX_WIDTH:, :]))
    else:
        x_ref, g_ref, wgu_ref, wd_ref, fg_ref, o_ref, act_ref = refs
        x = x_ref[...]
    h = _rmsnorm(x, g_ref[...]).astype(BF16)
    for c in range(D_FF // FF_CHUNK):
        lo = c * FF_CHUNK
        gate = _dot(h, wgu_ref[:, lo:lo + FF_CHUNK])
        up = _dot(h, wgu_ref[:, D_FF + lo:D_FF + lo + FF_CHUNK])
        act_ref[:, lo:lo + FF_CHUNK] = (gate * jax.nn.sigmoid(gate) * up).astype(BF16)
    y = x + 0.5 * _dot(act_ref[...], wd_ref[...])
    if final:
        y = _rmsnorm(y, fg_ref[...])
    o_ref[...] = y


def _ffn(x2d, g, wgu, wd, which, fg, final, mix=None):
    tokens = x2d.shape[0]
    tm = TOKEN_TILE if mix else FFN_TOKEN_TILE
    row = lambda i: (i, 0)
    weights = [_resident((1, D_MODEL)), _resident_layer(wgu, which),
               _resident_layer(wd, which), _resident((1, D_MODEL))]
    if mix:
        y_tok, y_mem, w_out, j = mix
        operands = (x2d, y_tok, y_mem, w_out, g, wgu, wd, fg)
        in_specs = [pl.BlockSpec((tm, D_MODEL), row), pl.BlockSpec((tm, FOX_WIDTH), row),
                    pl.BlockSpec((tm, MEM_WIDTH), row), _resident_layer(w_out, (j,))] + weights
    else:
        operands = (x2d, g, wgu, wd, fg)
        in_specs = [pl.BlockSpec((tm, D_MODEL), row)] + weights
    return pl.pallas_call(
        functools.partial(_ffn_kernel, final=final, mixed=bool(mix)),
        grid=(tokens // tm,),
        in_specs=in_specs,
        out_specs=pl.BlockSpec((tm, D_MODEL), row),
        out_shape=jax.ShapeDtypeStruct((tokens, D_MODEL), F32),
        scratch_shapes=[pltpu.VMEM((tm, D_FF), BF16)],
        compiler_params=_params(1),
        name=("ffn_mixed" if mix else "ffn") + ("_final" if final else ""),
    )(*operands)


def _mixer_a_kernel(x_ref, g_ref, win_ref, cw_ref, wout_ref, mkT_ref, mv_ref,
                    o_ref, tail_ref, *, tiles_per_seq):
    tm = x_ref.shape[0]
    c = CONV_DIM

    @pl.when(pl.program_id(0) % tiles_per_seq == 0)
    def _():
        tail_ref[...] = jnp.zeros_like(tail_ref)

    x = x_ref[...]
    h = _rmsnorm(x, g_ref[...]).astype(BF16)
    proj = {}

    def project(name, lo):
        def run():
            proj[name] = _dot(h, win_ref[:, lo:lo + c])
        return run

    y_mem = _mem_attn(_dot(h, win_ref[:, 3 * c:]), mkT_ref, mv_ref,
                      [project("gate_c", c), project("u", 2 * c), project("gate_b", 0)])
    v = proj["gate_c"] * proj["u"]

    prev = tail_ref[...]
    sub = lax.broadcasted_iota(jnp.int32, (V7X_SUBLANES, c), 0)
    shifted = []
    for k in range(1, CONV_WIDTH):
        rv = pltpu.roll(v, k, 0)
        head = jnp.where(sub < k, pltpu.roll(prev, k, 0), rv[:V7X_SUBLANES])
        shifted.append(jnp.concatenate([head, rv[V7X_SUBLANES:]], axis=0))
    tail_ref[...] = v[tm - V7X_SUBLANES:]

    cw = cw_ref[...]
    conv = cw[CONV_WIDTH - 1:CONV_WIDTH] * v
    for k in range(1, CONV_WIDTH):
        conv = conv + cw[CONV_WIDTH - 1 - k:CONV_WIDTH - k] * shifted[k - 1]
    y_tok = (proj["gate_b"] * conv).astype(BF16)
    o_ref[...] = x + _dot(y_tok, wout_ref[:c, :]) + _dot(y_mem, wout_ref[c:, :])


def _mixer_a(x2d, g, w_in, conv_w, w_out, mkT, mv, layer, seq):
    tokens = x2d.shape[0]
    tm = TOKEN_TILE
    tps = seq // tm
    row = lambda i: (i, 0)
    return pl.pallas_call(
        functools.partial(_mixer_a_kernel, tiles_per_seq=tps),
        grid=(tokens // tm,),
        in_specs=[
            pl.BlockSpec((tm, D_MODEL), row),
            _resident((1, D_MODEL)),
            _resident_layer(w_in, (layer,)),
            _resident_layer(conv_w, (layer,)),
            _resident_layer(w_out, (layer,)),
            pl.BlockSpec((1, 1, MEM_WIDTH, MEM_LEN), lambda i: (layer, i // tps, 0, 0)),
            pl.BlockSpec((1, 1, MEM_LEN, MEM_WIDTH), lambda i: (layer, i // tps, 0, 0)),
        ],
        out_specs=pl.BlockSpec((tm, D_MODEL), row),
        out_shape=jax.ShapeDtypeStruct((tokens, D_MODEL), F32),
        scratch_shapes=[pltpu.VMEM((V7X_SUBLANES, CONV_DIM), F32)],
        compiler_params=_params(1),
        name="mixer_a",
    )(x2d, g, w_in, conv_w, w_out, mkT, mv)


def _kvf_kernel(x_ref, g_ref, wkv_ref, wf_ref, bf_ref, tri_ref, sel_ref, hsum_ref,
                k_ref, bias_ref, vT_ref, c_ref, kn_ref, carry_ref, *, tiles_per_seq):
    tm = x_ref.shape[0]

    @pl.when(pl.program_id(0) % tiles_per_seq == 0)
    def _():
        carry_ref[...] = jnp.zeros_like(carry_ref)

    h = _rmsnorm(x_ref[...], g_ref[...]).astype(BF16)

    z = _dot(h, wf_ref[...]) + bf_ref[...]
    ls = -(jnp.maximum(-z, 0.0) + jnp.log1p(jnp.exp(-jnp.abs(z))))
    k = _dot(h, wkv_ref[:, :FOX_WIDTH])
    for hh in range(FOX_HEADS):
        k_ref[0, hh] = k[:, hh * HEAD_DIM:(hh + 1) * HEAD_DIM].astype(BF16)

    cs = _dot(tri_ref[...], jnp.concatenate(_split_bf16(ls), axis=1))
    cum = carry_ref[0:1, :]
    for p in range(N_SPLIT):
        cum = cum + cs[:, p * V7X_LANES:(p + 1) * V7X_LANES]
    carry_ref[...] = jnp.broadcast_to(cum[tm - 1:tm, :], carry_ref.shape)
    c_ref[0] = cum
    kn_ref[0] = jnp.broadcast_to(
        jnp.max(_dot((k * k).astype(BF16), hsum_ref[...]), axis=0, keepdims=True),
        kn_ref.shape[1:])
    v = _dot(h, wkv_ref[:, FOX_WIDTH:])

    bias = _dot(jnp.concatenate(_split_bf16(-LOG2E * cum), axis=1), sel_ref[...])
    lane = lax.broadcasted_iota(jnp.int32, (1, V7X_LANES), 1)
    ones_lanes = jnp.where((lane >= AUG_SHIFT) & (lane < AUG_SHIFT + N_SPLIT), 1.0, 0.0)
    bias_ref[0] = (bias + ones_lanes).astype(BF16)
    for hh in range(FOX_HEADS):
        sl = slice(hh * HEAD_DIM, (hh + 1) * HEAD_DIM)
        vT_ref[0, hh, :HEAD_DIM, :] = v[:, sl].T.astype(BF16)
    denom_rows = lax.broadcasted_iota(jnp.int32, (V_PAD_ROWS, tm), 0) == 0
    for hh in range(FOX_HEADS):
        vT_ref[0, hh, HEAD_DIM:, :] = jnp.where(denom_rows, 1.0, 0.0).astype(BF16)


def _kvf(x2d, g, w_kv, w_f, b_f, batch, seq):
    tokens = x2d.shape[0]
    tm = TOKEN_TILE
    tps = seq // tm
    tri = jnp.tril(jnp.ones((tm, tm), F32)).astype(BF16)
    sel = np.zeros((N_SPLIT * V7X_LANES, V7X_LANES), np.float32)
    for p in range(N_SPLIT):
        for hh in range(FOX_HEADS):
            sel[p * V7X_LANES + hh, N_SPLIT * hh + p] = 1.0
    sel = jnp.asarray(sel, BF16)
    hsum = _head_sum_matrix()
    return pl.pallas_call(
        functools.partial(_kvf_kernel, tiles_per_seq=tps),
        grid=(tokens // tm,),
        in_specs=[
            pl.BlockSpec((tm, D_MODEL), lambda i: (i, 0)),
            _resident((1, D_MODEL)),
            _resident(w_kv.shape),
            _resident(w_f.shape),
            _resident(b_f.shape),
            _resident(tri.shape),
            _resident(sel.shape),
            _resident(hsum.shape),
        ],
        out_specs=[
            pl.BlockSpec((1, FOX_HEADS, tm, HEAD_DIM), lambda i: (i // tps, 0, i % tps, 0)),
            pl.BlockSpec((1, tm, V7X_LANES), lambda i: (i // tps, i % tps, 0)),
            pl.BlockSpec((1, FOX_HEADS, V_ROWS, tm), lambda i: (i // tps, 0, 0, i % tps)),
            pl.BlockSpec((1, tm, V7X_LANES), lambda i: (i // tps, i % tps, 0)),
            pl.BlockSpec((1, V7X_SUBLANES, V7X_LANES), lambda i: (i, 0, 0)),
        ],
        out_shape=[
            jax.ShapeDtypeStruct((batch, FOX_HEADS, seq, HEAD_DIM), BF16),
            jax.ShapeDtypeStruct((batch, seq, V7X_LANES), BF16),
            jax.ShapeDtypeStruct((batch, FOX_HEADS, V_ROWS, seq), BF16),
            jax.ShapeDtypeStruct((batch, seq, V7X_LANES), F32),
            jax.ShapeDtypeStruct((tokens // tm, V7X_SUBLANES, V7X_LANES), F32),
        ],
        scratch_shapes=[pltpu.VMEM((V7X_SUBLANES, V7X_LANES), F32)],
        compiler_params=_params(1),
        name="kvf",
    )(x2d, g, w_kv, w_f, b_f, tri, sel, hsum)


def _qproj_kernel(x_ref, g_ref, wq_ref, mkT_ref, mv_ref, c_ref, kmax_ref,
                  qT_ref, ymem_ref, gap_ref):
    tm = x_ref.shape[0]
    h = _rmsnorm(x_ref[...], g_ref[...]).astype(BF16)
    cT = c_ref[0].T
    row = lax.broadcasted_iota(jnp.int32, (AUG_ROWS, tm), 0)
    gaps = []

    def finish_head(hh, q):
        qT = q.T
        reach =(BOUND_MARGIN * jnp.sqrt(jnp.sum(qT * qT, axis=0, keepdims=True))
                 * kmax_ref[0, hh:hh + 1, :])
        gaps.append(2.0 * reach)
        aug = jnp.where((row >= N_SPLIT * hh) & (row < N_SPLIT * (hh + 1)), 1.0, 0.0)
        for p, piece in enumerate(_split_bf16(LOG2E * cT[hh:hh + 1, :] - reach)):
            aug = jnp.where(row == AUG_SHIFT + p, piece.astype(F32), aug)
        qT_ref[0, hh, :HEAD_DIM, :] = qT.astype(BF16)
        qT_ref[0, hh, HEAD_DIM:HEAD_DIM + AUG_ROWS, :] = aug.astype(BF16)
        qT_ref[0, hh, HEAD_DIM + AUG_ROWS:, :] = jnp.zeros((HEAD_DIM - AUG_ROWS, tm), BF16)

    def project_pair(first):
        def run():
            q2 = (_dot(h, wq_ref[:, first * HEAD_DIM:(first + 2) * HEAD_DIM])
                  * (ATTN_SCALE * LOG2E))
            finish_head(first, q2[:, :HEAD_DIM])
            finish_head(first + 1, q2[:, HEAD_DIM:])
        return run

    ymem_ref[...] = _mem_attn(_dot(h, wq_ref[:, FOX_WIDTH:]), mkT_ref, mv_ref,
                              [project_pair(hh) for hh in range(0, FOX_HEADS, 2)])
    gap_ref[0] = jnp.broadcast_to(functools.reduce(jnp.maximum, gaps), gap_ref.shape[1:])


def _qproj(x2d, g, w_q, mkT, mv, c, kmax, layer, batch, seq):
    tokens = x2d.shape[0]
    tm = TOKEN_TILE
    tps = seq // tm
    return pl.pallas_call(
        _qproj_kernel,
        grid=(tokens // tm,),
        in_specs=[
            pl.BlockSpec((tm, D_MODEL), lambda i: (i, 0)),
            _resident((1, D_MODEL)),
            _resident_layer(w_q, (layer - N_A,)),
            pl.BlockSpec((1, 1, MEM_WIDTH, MEM_LEN), lambda i: (layer, i // tps, 0, 0)),
            pl.BlockSpec((1, 1, MEM_LEN, MEM_WIDTH), lambda i: (layer, i // tps, 0, 0)),
            pl.BlockSpec((1, tm, V7X_LANES), lambda i: (i // tps, i % tps, 0)),
            pl.BlockSpec((1, FOX_HEADS, tm), lambda i: (i // tps, 0, 0)),
        ],
        out_specs=[
            pl.BlockSpec((1, FOX_HEADS, 2 * HEAD_DIM, tm), lambda i: (i // tps, 0, 0, i % tps)),
            pl.BlockSpec((tm, MEM_WIDTH), lambda i: (i, 0)),
            pl.BlockSpec((1, V7X_SUBLANES, tm), lambda i: (i, 0, 0)),
        ],
        out_shape=[
            jax.ShapeDtypeStruct((batch, FOX_HEADS, 2 * HEAD_DIM, seq), BF16),
            jax.ShapeDtypeStruct((tokens, MEM_WIDTH), BF16),
            jax.ShapeDtypeStruct((tokens // tm, V7X_SUBLANES, tm), F32),
        ],
        compiler_params=_params(1),
        name="qproj",
    )(x2d, g, w_q, mkT, mv, c, kmax)


def _fox_kernel(qT_ref, k_hbm, bias_hbm, vT_hbm, o_ref, m_ref, acc_ref, s_ref,
                k_ref, bias_ref, vT_ref, kv_sem, *, online_max):
    t = ATTN_TILE
    tk = ATTN_KEY_TILE
    heads = qT_ref.shape[1]
    b, hg, qi = pl.program_id(0), pl.program_id(1), pl.program_id(2)
    q0 = pl.multiple_of(qi * t, t)

    def key_chunk_copies(chunk):
        rows = pl.ds(pl.multiple_of(chunk * t, t), t)
        heads_of_group = pl.ds(hg * heads, heads)
        slot = chunk % 2
        return (
            pltpu.make_async_copy(k_hbm.at[b, heads_of_group, rows, :],
                                  k_ref.at[:, rows, :], kv_sem.at[0, slot]),
            pltpu.make_async_copy(bias_hbm.at[b, rows, :], bias_ref.at[rows, :], kv_sem.at[1, slot]),
            pltpu.make_async_copy(vT_hbm.at[b, heads_of_group, :, rows],
                                  vT_ref.at[:, :, rows], kv_sem.at[2, slot]),
        )

    @pl.when(qi == 0)
    def _():
        for copy in key_chunk_copies(0):
            copy.start()

    @pl.when(qi + 1 < pl.num_programs(2))
    def _():
        for copy in key_chunk_copies(qi + 1):
            copy.start()

    for copy in key_chunk_copies(qi):
        copy.wait()

    if online_max:
        m_ref[...] = jnp.full_like(m_ref, -jnp.inf)
    acc_ref[...] = jnp.zeros_like(acc_ref)

    def logits(unit):
        g, k0, diag_offset = unit
        ka = jnp.concatenate([k_ref[g, pl.ds(k0, tk), :], bias_ref[pl.ds(k0, tk), :]],
                             axis=1)
        return _dot(ka, qT_ref[0, g, :, (diag_offset or 0):])

    def softmax_step(unit, s):
        g, k0, diag_offset = unit
        lo = diag_offset or 0
        if diag_offset is not None:
            causal = (lax.broadcasted_iota(jnp.int32, s.shape, 0)
                      <= lax.broadcasted_iota(jnp.int32, s.shape, 1))
            s = jnp.where(causal, s, NEG_INF)
        vt = vT_ref[g, :, pl.ds(k0, tk)]
        if online_max:
            m_prev = m_ref[g, :, lo:]
            m_new = jnp.maximum(m_prev, jnp.max(s, axis=0, keepdims=True))
            alpha = jnp.exp2(m_prev - m_new)
            acc_ref[g, :, lo:] = (alpha * acc_ref[g, :, lo:]
                                  + _dot(vt, jnp.exp2(s - m_new).astype(BF16)))
            m_ref[g, :, lo:] = m_new
        else:
            acc_ref[g, :, lo:] += _dot(vt, jnp.exp2(s).astype(BF16))

    def first_units(k0):
        return [(g, k0, None) for g in range(ATTN_LOOKAHEAD)]

    def run(units, following):
        pending = [s_ref[j] for j in range(ATTN_LOOKAHEAD)]
        seq = units + following
        for i, u in enumerate(units):
            s = pending.pop(0)
            if i + ATTN_LOOKAHEAD < len(seq):
                pending.append(logits(seq[i + ATTN_LOOKAHEAD]))
            softmax_step(u, s)
        for j, s in enumerate(pending):
            s_ref[j] = s

    def full_keys(base, steps):
        run([(g, pl.multiple_of(base + j * tk, tk), None)
             for j in range(steps) for g in range(heads)],
            first_units(pl.multiple_of(base + steps * tk, tk)))

    tile_steps = t // tk

    def body(ki, c):
        full_keys(ki * (2 * t), 2 * tile_steps)
        return c

    for j, u in enumerate(first_units(0)):
        s_ref[j] = logits(u)
    lax.fori_loop(0, qi // 2, body, 0)

    @pl.when(qi % 2 == 1)
    def _():
        full_keys((qi - 1) * t, tile_steps)

    run([(g, pl.multiple_of(q0 + d * tk, tk), d * tk)
         for d in range(t // tk) for g in range(heads)], [])
    for g in range(heads):
        out = acc_ref[g, :HEAD_DIM, :] / acc_ref[g, HEAD_DIM:HEAD_DIM + 1, :]
        o_ref[0, :, g * HEAD_DIM:(g + 1) * HEAD_DIM] = out.T.astype(BF16)


def _fox_attention(qT, k, bias, vT, online_max):
    batch, heads, _, seq = qT.shape
    t = ATTN_TILE
    hg = ATTN_HEADS
    assert ATTN_LOOKAHEAD <= hg and t % ATTN_KEY_TILE == 0
    hbm = pl.BlockSpec(memory_space=pl.ANY)
    return pl.pallas_call(
        functools.partial(_fox_kernel, online_max=online_max),
        grid=(batch, heads // hg, seq // t),
        in_specs=[
            pl.BlockSpec((1, hg, 2 * HEAD_DIM, t), lambda b, h, q: (b, h, 0, q)),
            hbm, hbm, hbm,
        ],
        out_specs=pl.BlockSpec((1, t, hg * HEAD_DIM), lambda b, h, q: (b, q, h)),
        out_shape=jax.ShapeDtypeStruct((batch, seq, heads * HEAD_DIM), BF16),
        scratch_shapes=[
            pltpu.VMEM((hg, 1, t), F32),
            pltpu.VMEM((hg, V_ROWS, t), F32),
            pltpu.VMEM((ATTN_LOOKAHEAD, ATTN_KEY_TILE, t), F32),
            pltpu.VMEM((hg, seq, HEAD_DIM), BF16),
            pltpu.VMEM((seq, V7X_LANES), BF16),
            pltpu.VMEM((hg, V_ROWS, seq), BF16),
            pltpu.SemaphoreType.DMA((3, 2)),
        ],
        compiler_params=_params(3),
        name="fox_attention_online" if online_max else "fox_attention_bounded",
    )(qT, k, bias, vT)


def kernel(x, mem, ffn_norm, ffn_w_gate_up, ffn_w_down, mix_norm, mem_norm, mem_w_kv,
           a_w_in, a_conv_w, a_w_out, kv_norm, w_kvf, b_f, b_w_q, b_w_out, final_norm):
    batch, seq, d = x.shape
    assert d == D_MODEL and seq % TOKEN_TILE == 0 and seq % ATTN_TILE == 0
    assert mem.shape == (batch, MEM_LEN, D_MODEL)
    tokens = batch * seq

    wgu = ffn_w_gate_up.astype(BF16)
    wd = ffn_w_down.astype(BF16)
    w_mem = mem_w_kv.astype(BF16)
    w_in = a_w_in.astype(BF16)
    w_out_a = a_w_out.astype(BF16)
    w_kv = w_kvf[:, :2 * FOX_WIDTH].astype(BF16)
    w_f = jnp.pad(w_kvf[:, 2 * FOX_WIDTH:], ((0, 0), (0, V7X_LANES - FOX_HEADS))).astype(BF16)
    b_f_row = jnp.pad(b_f, (0, V7X_LANES - FOX_HEADS)).reshape(1, V7X_LANES)
    w_q = b_w_q.astype(BF16)
    w_out_b = b_w_out.astype(BF16)
    fg = final_norm.reshape(1, D_MODEL)

    mkT, mv = _mem_kv(mem.reshape(batch * MEM_LEN, D_MODEL), mem_norm.reshape(1, D_MODEL),
                      w_mem, batch)

    h = x.reshape(tokens, D_MODEL)
    k_sh = bias_sh = vT = c_sh = k_max = None
    for l in range(DEPTH):
        if l == N_A:
            k_sh, bias_sh, vT, c_sh, kn = _kvf(h, kv_norm.reshape(1, D_MODEL), w_kv, w_f, b_f_row,
                                               batch, seq)
            k_max = jnp.sqrt(jnp.max(kn[:, 0, :FOX_HEADS].reshape(batch, -1, FOX_HEADS), axis=1))
            k_max = jnp.broadcast_to(k_max[:, :, None], (batch, FOX_HEADS, TOKEN_TILE))
        h = _ffn(h, ffn_norm[l, 0].reshape(1, D_MODEL), wgu, wd, (l, 0), fg, False)
        g_mix = mix_norm[l].reshape(1, D_MODEL)
        mix = None
        if l < N_A:
            h = _mixer_a(h, g_mix, w_in, a_conv_w, w_out_a, mkT, mv, l, seq)
        else:
            j = l - N_A
            qT, y_mem, gap = _qproj(h, g_mix, w_q, mkT, mv, c_sh, k_max, l, batch, seq)
            y_tok = lax.cond(
                jnp.max(gap) < MAX_SHIFT_GAP,
                functools.partial(_fox_attention, online_max=False),
                functools.partial(_fox_attention, online_max=True),
                qT, k_sh, bias_sh, vT).reshape(tokens, FOX_WIDTH)
            mix = (y_tok, y_mem, w_out_b, j)
        h = _ffn(h, ffn_norm[l, 1].reshape(1, D_MODEL), wgu, wd, (l, 1), fg, l == DEPTH - 1, mix)
    return h.reshape(batch, seq, D_MODEL)
```

```python
import functools
import math

import jax
import jax.numpy as jnp
import numpy as np
from jax import lax
from jax.experimental import pallas as pl
from jax.experimental.pallas import tpu as pltpu

D_MODEL = 1024
DEPTH = 4
N_A = DEPTH // 2
MEM_LEN = 256
MEM_HEADS = 4
HEAD_DIM = 128
MEM_WIDTH = MEM_HEADS * HEAD_DIM
CONV_DIM = D_MODEL
CONV_WIDTH = 3
FOX_HEADS = 8
FOX_WIDTH = FOX_HEADS * HEAD_DIM
D_FF = 2816
RMS_EPS = 1e-6
NEG_INF = float(np.finfo(np.float32).min)
ATTN_SCALE = HEAD_DIM ** -0.5
LOG2E = math.log2(math.e)

V7X_LANES = 128
V7X_SUBLANES = 8
V7X_MXU_DIM = 256
V7X_VMEM_LIMIT_BYTES = 60000 * 1024

TOKEN_TILE = 512
WIDE_TOKEN_TILE = 1024
FF_CHUNK = V7X_MXU_DIM
ATTN_TILE = 512
ATTN_KEY_TILE = 256
ATTN_LOOKAHEAD = 2
ATTN_HEADS = 8
N_SPLIT = 3
V_PAD_ROWS = 16
V_ROWS = HEAD_DIM + V_PAD_ROWS
AUG_SHIFT = N_SPLIT * FOX_HEADS
AUG_ROWS = 32
assert AUG_SHIFT + N_SPLIT <= AUG_ROWS <= HEAD_DIM
BOUND_MARGIN = 1.02
MAX_SHIFT_GAP = 100.0

F32 = jnp.float32
BF16 = jnp.bfloat16


def _dot(a, b):
    return jnp.dot(a, b, preferred_element_type=F32)


def _rmsnorm(x, g):
    y = x * lax.rsqrt(jnp.mean(x * x, axis=-1, keepdims=True) + RMS_EPS)
    return y * g


def _split_bf16(x):
    pieces = []
    r = x
    for _ in range(N_SPLIT):
        p = r.astype(BF16)
        pieces.append(p)
        r = r - p.astype(F32)
    return pieces


def _head_sum_matrix():
    m = np.zeros((FOX_WIDTH, V7X_LANES), np.float32)
    for hh in range(FOX_HEADS):
        m[hh * HEAD_DIM:(hh + 1) * HEAD_DIM, hh] = 1.0
    return jnp.asarray(m, BF16)


def _params(n_grid_dims):
    return pltpu.CompilerParams(
        dimension_semantics=("arbitrary",) * n_grid_dims,
        vmem_limit_bytes=V7X_VMEM_LIMIT_BYTES)


def _resident(shape):
    zeros = (0,) * len(shape)
    return pl.BlockSpec(shape, lambda *_: zeros, pipeline_mode=pl.Buffered(1))


def _resident_layer(stacked, index):
    n = len(index)
    block_index = tuple(index) + (0,) * (stacked.ndim - n)
    return pl.BlockSpec((None,) * n + tuple(stacked.shape[n:]), lambda *_: block_index,
                        pipeline_mode=pl.Buffered(1))


def _mem_kv_kernel(mem_ref, g_ref, w_ref, mkT_ref, mv_ref, *, batch):
    h = _rmsnorm(mem_ref[...], g_ref[...]).astype(BF16)
    mkv = _dot(h, w_ref[0])
    for b in range(batch):
        blk = mkv[b * MEM_LEN:(b + 1) * MEM_LEN]
        mkT_ref[0, b] = blk[:, :MEM_WIDTH].T.astype(BF16)
        mv_ref[0, b] = blk[:, MEM_WIDTH:].astype(BF16)


def _mem_kv(mem2d, mem_norm, w_kv, batch):
    rows = mem2d.shape[0]
    return pl.pallas_call(
        functools.partial(_mem_kv_kernel, batch=batch),
        grid=(DEPTH,),
        in_specs=[
            pl.BlockSpec((rows, D_MODEL), lambda l: (0, 0)),
            pl.BlockSpec((1, D_MODEL), lambda l: (0, 0)),
            pl.BlockSpec((1, D_MODEL, 2 * MEM_WIDTH), lambda l: (l, 0, 0)),
        ],
        out_specs=[
            pl.BlockSpec((1, batch, MEM_WIDTH, MEM_LEN), lambda l: (l, 0, 0, 0)),
            pl.BlockSpec((1, batch, MEM_LEN, MEM_WIDTH), lambda l: (l, 0, 0, 0)),
        ],
        out_shape=[
            jax.ShapeDtypeStruct((DEPTH, batch, MEM_WIDTH, MEM_LEN), BF16),
            jax.ShapeDtypeStruct((DEPTH, batch, MEM_LEN, MEM_WIDTH), BF16),
        ],
        compiler_params=_params(1),
        name="mem_kv",
    )(mem2d, mem_norm, w_kv)


def _mem_attn(qm, mkT_ref, mv_ref, fillers=()):
    fillers = list(fillers)

    def logits(hh):
        sl = slice(hh * HEAD_DIM, (hh + 1) * HEAD_DIM)
        return _dot(qm[:, sl].astype(BF16), mkT_ref[0, 0, sl, :]) * ATTN_SCALE

    outs = []
    s_next = logits(0)
    for hh in range(MEM_HEADS):
        s = s_next
        if hh + 1 < MEM_HEADS:
            s_next = logits(hh + 1)
        if fillers:
            fillers.pop(0)()
        p = jnp.exp(s - jnp.max(s, axis=-1, keepdims=True))
        denom = jnp.sum(p, axis=-1, keepdims=True)
        outs.append(_dot(p.astype(BF16), mv_ref[0, 0, :, hh * HEAD_DIM:(hh + 1) * HEAD_DIM]) / denom)
    for f in fillers:
        f()
    return jnp.concatenate(outs, axis=-1).astype(BF16)


def _ffn_kernel(*refs, final, mixed):
    if mixed:
        x_ref, yt_ref, ym_ref, wout_ref, g_ref, wgu_ref, wd_ref, fg_ref, o_ref, act_ref = refs
        x = (x_ref[...] + _dot(yt_ref[...], wout_ref[:FOX_WIDTH, :])
             + _dot(ym_ref[...], wout_ref[FOX_WIDTH:, :]))
    else:
        x_ref, g_ref, wgu_ref, wd_ref, fg_ref, o_ref, act_ref = refs
        x = x_ref[...]
    h = _rmsnorm(x, g_ref[...]).astype(BF16)
    for c in range(D_FF // FF_CHUNK):
        lo = c * FF_CHUNK
        gate = _dot(h, wgu_ref[:, lo:lo + FF_CHUNK])
        up = _dot(h, wgu_ref[:, D_FF + lo:D_FF + lo + FF_CHUNK])
        act_ref[:, lo:lo + FF_CHUNK] = (gate * jax.nn.sigmoid(gate) * up).astype(BF16)
    y = x + 0.5 * _dot(act_ref[...], wd_ref[...])
    if final:
        y = _rmsnorm(y, fg_ref[...])
    o_ref[...] = y


def _ffn(x2d, g, wgu, wd, which, fg, final, mix=None):
    tokens = x2d.shape[0]
    tm = TOKEN_TILE if mix else WIDE_TOKEN_TILE
    row = lambda i: (i, 0)
    weights = [_resident((1, D_MODEL)), _resident_layer(wgu, which),
               _resident_layer(wd, which), _resident((1, D_MODEL))]
    if mix:
        y_tok, y_mem, w_out, j = mix
        operands = (x2d, y_tok, y_mem, w_out, g, wgu, wd, fg)
        in_specs = [pl.BlockSpec((tm, D_MODEL), row), pl.BlockSpec((tm, FOX_WIDTH), row),
                    pl.BlockSpec((tm, MEM_WIDTH), row), _resident_layer(w_out, (j,))] + weights
    else:
        operands = (x2d, g, wgu, wd, fg)
        in_specs = [pl.BlockSpec((tm, D_MODEL), row)] + weights
    return pl.pallas_call(
        functools.partial(_ffn_kernel, final=final, mixed=bool(mix)),
        grid=(tokens // tm,),
        in_specs=in_specs,
        out_specs=pl.BlockSpec((tm, D_MODEL), row),
        out_shape=jax.ShapeDtypeStruct((tokens, D_MODEL), F32),
        scratch_shapes=[pltpu.VMEM((tm, D_FF), BF16)],
        compiler_params=_params(1),
        name=("ffn_mixed" if mix else "ffn") + ("_final" if final else ""),
    )(*operands)


def _mixer_a_kernel(x_ref, g_ref, win_ref, cw_ref, wout_ref, mkT_ref, mv_ref,
                    o_ref, tail_ref, *, tiles_per_seq):
    tm = x_ref.shape[0]
    c = CONV_DIM

    @pl.when(pl.program_id(0) % tiles_per_seq == 0)
    def _():
        tail_ref[...] = jnp.zeros_like(tail_ref)

    x = x_ref[...]
    h = _rmsnorm(x, g_ref[...]).astype(BF16)
    proj = {}

    def project(name, lo):
        def run():
            proj[name] = _dot(h, win_ref[:, lo:lo + c])
        return run

    y_mem = _mem_attn(_dot(h, win_ref[:, 3 * c:]), mkT_ref, mv_ref,
                      [project("gate_c", c), project("u", 2 * c), project("gate_b", 0)])
    v = proj["gate_c"] * proj["u"]

    prev = tail_ref[...]
    sub = lax.broadcasted_iota(jnp.int32, (V7X_SUBLANES, c), 0)
    shifted = []
    for k in range(1, CONV_WIDTH):
        rv = pltpu.roll(v, k, 0)
        head = jnp.where(sub < k, pltpu.roll(prev, k, 0), rv[:V7X_SUBLANES])
        shifted.append(jnp.concatenate([head, rv[V7X_SUBLANES:]], axis=0))
    tail_ref[...] = v[tm - V7X_SUBLANES:]

    cw = cw_ref[...]
    conv = cw[CONV_WIDTH - 1:CONV_WIDTH] * v
    for k in range(1, CONV_WIDTH):
        conv = conv + cw[CONV_WIDTH - 1 - k:CONV_WIDTH - k] * shifted[k - 1]
    y_tok = (proj["gate_b"] * conv).astype(BF16)
    o_ref[...] = x + _dot(y_tok, wout_ref[:c, :]) + _dot(y_mem, wout_ref[c:, :])


def _mixer_a(x2d, g, w_in, conv_w, w_out, mkT, mv, layer, seq):
    tokens = x2d.shape[0]
    tm = TOKEN_TILE
    tps = seq // tm
    row = lambda i: (i, 0)
    return pl.pallas_call(
        functools.partial(_mixer_a_kernel, tiles_per_seq=tps),
        grid=(tokens // tm,),
        in_specs=[
            pl.BlockSpec((tm, D_MODEL), row),
            _resident((1, D_MODEL)),
            _resident_layer(w_in, (layer,)),
            _resident_layer(conv_w, (layer,)),
            _resident_layer(w_out, (layer,)),
            pl.BlockSpec((1, 1, MEM_WIDTH, MEM_LEN), lambda i: (layer, i // tps, 0, 0)),
            pl.BlockSpec((1, 1, MEM_LEN, MEM_WIDTH), lambda i: (layer, i // tps, 0, 0)),
        ],
        out_specs=pl.BlockSpec((tm, D_MODEL), row),
        out_shape=jax.ShapeDtypeStruct((tokens, D_MODEL), F32),
        scratch_shapes=[pltpu.VMEM((V7X_SUBLANES, CONV_DIM), F32)],
        compiler_params=_params(1),
        name="mixer_a",
    )(x2d, g, w_in, conv_w, w_out, mkT, mv)


def _kvf_kernel(x_ref, g_ref, wkv_ref, wf_ref, bf_ref, tri_ref, sel_ref, hsum_ref,
                k_ref, bias_ref, vT_ref, c_ref, kn_ref, carry_ref, *, tiles_per_seq):
    tm = x_ref.shape[0]

    @pl.when(pl.program_id(0) % tiles_per_seq == 0)
    def _():
        carry_ref[...] = jnp.zeros_like(carry_ref)

    h = _rmsnorm(x_ref[...], g_ref[...]).astype(BF16)

    z = _dot(h, wf_ref[...]) + bf_ref[...]
    ls = -(jnp.maximum(-z, 0.0) + jnp.log1p(jnp.exp(-jnp.abs(z))))
    k = _dot(h, wkv_ref[:, :FOX_WIDTH])
    for hh in range(FOX_HEADS):
        k_ref[0, hh] = k[:, hh * HEAD_DIM:(hh + 1) * HEAD_DIM].astype(BF16)

    cs = _dot(tri_ref[...], jnp.concatenate(_split_bf16(ls), axis=1))
    cum = carry_ref[0:1, :]
    for p in range(N_SPLIT):
        cum = cum + cs[:, p * V7X_LANES:(p + 1) * V7X_LANES]
    carry_ref[...] = jnp.broadcast_to(cum[tm - 1:tm, :], carry_ref.shape)
    c_ref[0] = cum
    kn_ref[0] = jnp.broadcast_to(
        jnp.max(_dot((k * k).astype(BF16), hsum_ref[...]), axis=0, keepdims=True),
        kn_ref.shape[1:])
    v = _dot(h, wkv_ref[:, FOX_WIDTH:])

    bias = _dot(jnp.concatenate(_split_bf16(-LOG2E * cum), axis=1), sel_ref[...])
    lane = lax.broadcasted_iota(jnp.int32, (1, V7X_LANES), 1)
    ones_lanes = jnp.where((lane >= AUG_SHIFT) & (lane < AUG_SHIFT + N_SPLIT), 1.0, 0.0)
    bias_ref[0] = (bias + ones_lanes).astype(BF16)
    for hh in range(FOX_HEADS):
        sl = slice(hh * HEAD_DIM, (hh + 1) * HEAD_DIM)
        vT_ref[0, hh, :HEAD_DIM, :] = v[:, sl].T.astype(BF16)
    denom_rows = lax.broadcasted_iota(jnp.int32, (V_PAD_ROWS, tm), 0) == 0
    for hh in range(FOX_HEADS):
        vT_ref[0, hh, HEAD_DIM:, :] = jnp.where(denom_rows, 1.0, 0.0).astype(BF16)


def _kvf(x2d, g, w_kv, w_f, b_f, batch, seq):
    tokens = x2d.shape[0]
    tm = TOKEN_TILE
    tps = seq // tm
    tri = jnp.tril(jnp.ones((tm, tm), F32)).astype(BF16)
    sel = np.zeros((N_SPLIT * V7X_LANES, V7X_LANES), np.float32)
    for p in range(N_SPLIT):
        for hh in range(FOX_HEADS):
            sel[p * V7X_LANES + hh, N_SPLIT * hh + p] = 1.0
    sel = jnp.asarray(sel, BF16)
    hsum = _head_sum_matrix()
    return pl.pallas_call(
        functools.partial(_kvf_kernel, tiles_per_seq=tps),
        grid=(tokens // tm,),
        in_specs=[
            pl.BlockSpec((tm, D_MODEL), lambda i: (i, 0)),
            _resident((1, D_MODEL)),
            _resident(w_kv.shape),
            _resident(w_f.shape),
            _resident(b_f.shape),
            _resident(tri.shape),
            _resident(sel.shape),
            _resident(hsum.shape),
        ],
        out_specs=[
            pl.BlockSpec((1, FOX_HEADS, tm, HEAD_DIM), lambda i: (i // tps, 0, i % tps, 0)),
            pl.BlockSpec((1, tm, V7X_LANES), lambda i: (i // tps, i % tps, 0)),
            pl.BlockSpec((1, FOX_HEADS, V_ROWS, tm), lambda i: (i // tps, 0, 0, i % tps)),
            pl.BlockSpec((1, tm, V7X_LANES), lambda i: (i // tps, i % tps, 0)),
            pl.BlockSpec((1, V7X_SUBLANES, V7X_LANES), lambda i: (i, 0, 0)),
        ],
        out_shape=[
            jax.ShapeDtypeStruct((batch, FOX_HEADS, seq, HEAD_DIM), BF16),
            jax.ShapeDtypeStruct((batch, seq, V7X_LANES), BF16),
            jax.ShapeDtypeStruct((batch, FOX_HEADS, V_ROWS, seq), BF16),
            jax.ShapeDtypeStruct((batch, seq, V7X_LANES), F32),
            jax.ShapeDtypeStruct((tokens // tm, V7X_SUBLANES, V7X_LANES), F32),
        ],
        scratch_shapes=[pltpu.VMEM((V7X_SUBLANES, V7X_LANES), F32)],
        compiler_params=_params(1),
        name="kvf",
    )(x2d, g, w_kv, w_f, b_f, tri, sel, hsum)


def _qproj_kernel(x_ref, g_ref, wq_ref, mkT_ref, mv_ref, c_ref, kmax_ref,
                  qT_ref, ymem_ref, gap_ref):
    tm = x_ref.shape[0]
    h = _rmsnorm(x_ref[...], g_ref[...]).astype(BF16)
    cT = c_ref[0].T
    row = lax.broadcasted_iota(jnp.int32, (AUG_ROWS, tm), 0)
    gaps = []

    def finish_head(hh, q):
        qT = q.T
        reach =(BOUND_MARGIN * jnp.sqrt(jnp.sum(qT * qT, axis=0, keepdims=True))
                 * kmax_ref[0, hh:hh + 1, :])
        gaps.append(2.0 * reach)
        aug = jnp.where((row >= N_SPLIT * hh) & (row < N_SPLIT * (hh + 1)), 1.0, 0.0)
        for p, piece in enumerate(_split_bf16(LOG2E * cT[hh:hh + 1, :] - reach)):
            aug = jnp.where(row == AUG_SHIFT + p, piece.astype(F32), aug)
        qT_ref[0, hh, :HEAD_DIM, :] = qT.astype(BF16)
        qT_ref[0, hh, HEAD_DIM:HEAD_DIM + AUG_ROWS, :] = aug.astype(BF16)
        qT_ref[0, hh, HEAD_DIM + AUG_ROWS:, :] = jnp.zeros((HEAD_DIM - AUG_ROWS, tm), BF16)

    def project_pair(first):
        def run():
            q2 = (_dot(h, wq_ref[:, first * HEAD_DIM:(first + 2) * HEAD_DIM])
                  * (ATTN_SCALE * LOG2E))
            finish_head(first, q2[:, :HEAD_DIM])
            finish_head(first + 1, q2[:, HEAD_DIM:])
        return run

    ymem_ref[...] = _mem_attn(_dot(h, wq_ref[:, FOX_WIDTH:]), mkT_ref, mv_ref,
                              [project_pair(hh) for hh in range(0, FOX_HEADS, 2)])
    gap_ref[0] = jnp.broadcast_to(functools.reduce(jnp.maximum, gaps), gap_ref.shape[1:])


def _qproj(x2d, g, w_q, mkT, mv, c, kmax, layer, batch, seq):
    tokens = x2d.shape[0]
    tm = WIDE_TOKEN_TILE
    tps = seq // tm
    return pl.pallas_call(
        _qproj_kernel,
        grid=(tokens // tm,),
        in_specs=[
            pl.BlockSpec((tm, D_MODEL), lambda i: (i, 0)),
            _resident((1, D_MODEL)),
            _resident_layer(w_q, (layer - N_A,)),
            pl.BlockSpec((1, 1, MEM_WIDTH, MEM_LEN), lambda i: (layer, i // tps, 0, 0)),
            pl.BlockSpec((1, 1, MEM_LEN, MEM_WIDTH), lambda i: (layer, i // tps, 0, 0)),
            pl.BlockSpec((1, tm, V7X_LANES), lambda i: (i // tps, i % tps, 0)),
            pl.BlockSpec((1, FOX_HEADS, tm), lambda i: (i // tps, 0, 0)),
        ],
        out_specs=[
            pl.BlockSpec((1, FOX_HEADS, 2 * HEAD_DIM, tm), lambda i: (i // tps, 0, 0, i % tps)),
            pl.BlockSpec((tm, MEM_WIDTH), lambda i: (i, 0)),
            pl.BlockSpec((1, V7X_SUBLANES, tm), lambda i: (i, 0, 0)),
        ],
        out_shape=[
            jax.ShapeDtypeStruct((batch, FOX_HEADS, 2 * HEAD_DIM, seq), BF16),
            jax.ShapeDtypeStruct((tokens, MEM_WIDTH), BF16),
            jax.ShapeDtypeStruct((tokens // tm, V7X_SUBLANES, tm), F32),
        ],
        compiler_params=_params(1),
        name="qproj",
    )(x2d, g, w_q, mkT, mv, c, kmax)


def _fox_kernel(qT_ref, k_hbm, bias_hbm, vT_hbm, o_ref, m_ref, acc_ref, s_ref,
                k_ref, bias_ref, vT_ref, kv_sem, *, online_max):
    t = ATTN_TILE
    tk = ATTN_KEY_TILE
    heads = qT_ref.shape[1]
    b, hg, qi = pl.program_id(0), pl.program_id(1), pl.program_id(2)
    q0 = pl.multiple_of(qi * t, t)

    def key_chunk_copies(chunk):
        rows = pl.ds(pl.multiple_of(chunk * t, t), t)
        heads_of_group = pl.ds(hg * heads, heads)
        slot = chunk % 2
        return (
            pltpu.make_async_copy(k_hbm.at[b, heads_of_group, rows, :],
                                  k_ref.at[:, rows, :], kv_sem.at[0, slot]),
            pltpu.make_async_copy(bias_hbm.at[b, rows, :], bias_ref.at[rows, :], kv_sem.at[1, slot]),
            pltpu.make_async_copy(vT_hbm.at[b, heads_of_group, :, rows],
                                  vT_ref.at[:, :, rows], kv_sem.at[2, slot]),
        )

    @pl.when(qi == 0)
    def _():
        for copy in key_chunk_copies(0):
            copy.start()

    @pl.when(qi + 1 < pl.num_programs(2))
    def _():
        for copy in key_chunk_copies(qi + 1):
            copy.start()

    for copy in key_chunk_copies(qi):
        copy.wait()

    if online_max:
        m_ref[...] = jnp.full_like(m_ref, -jnp.inf)
    acc_ref[...] = jnp.zeros_like(acc_ref)

    def logits(unit):
        g, k0, diag_offset = unit
        ka = jnp.concatenate([k_ref[g, pl.ds(k0, tk), :], bias_ref[pl.ds(k0, tk), :]],
                             axis=1)
        return _dot(ka, qT_ref[0, g, :, (diag_offset or 0):])

    def softmax_step(unit, s):
        g, k0, diag_offset = unit
        lo = diag_offset or 0
        if diag_offset is not None:
            causal = (lax.broadcasted_iota(jnp.int32, s.shape, 0)
                      <= lax.broadcasted_iota(jnp.int32, s.shape, 1))
            s = jnp.where(causal, s, NEG_INF)
        vt = vT_ref[g, :, pl.ds(k0, tk)]
        if online_max:
            m_prev = m_ref[g, :, lo:]
            m_new = jnp.maximum(m_prev, jnp.max(s, axis=0, keepdims=True))
            alpha = jnp.exp2(m_prev - m_new)
            acc_ref[g, :, lo:] = (alpha * acc_ref[g, :, lo:]
                                  + _dot(vt, jnp.exp2(s - m_new).astype(BF16)))
            m_ref[g, :, lo:] = m_new
        else:
            acc_ref[g, :, lo:] += _dot(vt, jnp.exp2(s).astype(BF16))

    def first_units(k0):
        return [(g, k0, None) for g in range(ATTN_LOOKAHEAD)]

    def run(units, following):
        pending = [s_ref[j] for j in range(ATTN_LOOKAHEAD)]
        seq = units + following
        for i, u in enumerate(units):
            s = pending.pop(0)
            if i + ATTN_LOOKAHEAD < len(seq):
                pending.append(logits(seq[i + ATTN_LOOKAHEAD]))
            softmax_step(u, s)
        for j, s in enumerate(pending):
            s_ref[j] = s

    def full_keys(base, steps):
        run([(g, pl.multiple_of(base + j * tk, tk), None)
             for j in range(steps) for g in range(heads)],
            first_units(pl.multiple_of(base + steps * tk, tk)))

    tile_steps = t // tk

    def body(ki, c):
        full_keys(ki * (2 * t), 2 * tile_steps)
        return c

    for j, u in enumerate(first_units(0)):
        s_ref[j] = logits(u)
    lax.fori_loop(0, qi // 2, body, 0)

    @pl.when(qi % 2 == 1)
    def _():
        full_keys((qi - 1) * t, tile_steps)

    run([(g, pl.multiple_of(q0 + d * tk, tk), d * tk)
         for d in range(t // tk) for g in range(heads)], [])
    for g in range(heads):
        out = acc_ref[g, :HEAD_DIM, :] / acc_ref[g, HEAD_DIM:HEAD_DIM + 1, :]
        o_ref[0, :, g * HEAD_DIM:(g + 1) * HEAD_DIM] = out.T.astype(BF16)


def _fox_attention(qT, k, bias, vT, online_max):
    batch, heads, _, seq = qT.shape
    t = ATTN_TILE
    hg = ATTN_HEADS
    assert ATTN_LOOKAHEAD <= hg and t % ATTN_KEY_TILE == 0
    hbm = pl.BlockSpec(memory_space=pl.ANY)
    return pl.pallas_call(
        functools.partial(_fox_kernel, online_max=online_max),
        grid=(batch, heads // hg, seq // t),
        in_specs=[
            pl.BlockSpec((1, hg, 2 * HEAD_DIM, t), lambda b, h, q: (b, h, 0, q)),
            hbm, hbm, hbm,
        ],
        out_specs=pl.BlockSpec((1, t, hg * HEAD_DIM), lambda b, h, q: (b, q, h)),
        out_shape=jax.ShapeDtypeStruct((batch, seq, heads * HEAD_DIM), BF16),
        scratch_shapes=[
            pltpu.VMEM((hg, 1, t), F32),
            pltpu.VMEM((hg, V_ROWS, t), F32),
            pltpu.VMEM((ATTN_LOOKAHEAD, ATTN_KEY_TILE, t), F32),
            pltpu.VMEM((hg, seq, HEAD_DIM), BF16),
            pltpu.VMEM((seq, V7X_LANES), BF16),
            pltpu.VMEM((hg, V_ROWS, seq), BF16),
            pltpu.SemaphoreType.DMA((3, 2)),
        ],
        compiler_params=_params(3),
        name="fox_attention_online" if online_max else "fox_attention_bounded",
    )(qT, k, bias, vT)


def kernel(x, mem, ffn_norm, ffn_w_gate_up, ffn_w_down, mix_norm, mem_norm, mem_w_kv,
           a_w_in, a_conv_w, a_w_out, kv_norm, w_kvf, b_f, b_w_q, b_w_out, final_norm):
    batch, seq, d = x.shape
    assert d == D_MODEL and seq % TOKEN_TILE == 0 and seq % ATTN_TILE == 0
    assert mem.shape == (batch, MEM_LEN, D_MODEL)
    tokens = batch * seq

    wgu = ffn_w_gate_up.astype(BF16)
    wd = ffn_w_down.astype(BF16)
    w_mem = mem_w_kv.astype(BF16)
    w_in = a_w_in.astype(BF16)
    w_out_a = a_w_out.astype(BF16)
    w_kv = w_kvf[:, :2 * FOX_WIDTH].astype(BF16)
    w_f = jnp.pad(w_kvf[:, 2 * FOX_WIDTH:], ((0, 0), (0, V7X_LANES - FOX_HEADS))).astype(BF16)
    b_f_row = jnp.pad(b_f, (0, V7X_LANES - FOX_HEADS)).reshape(1, V7X_LANES)
    w_q = b_w_q.astype(BF16)
    w_out_b = b_w_out.astype(BF16)
    fg = final_norm.reshape(1, D_MODEL)

    mkT, mv = _mem_kv(mem.reshape(batch * MEM_LEN, D_MODEL), mem_norm.reshape(1, D_MODEL),
                      w_mem, batch)

    h = x.reshape(tokens, D_MODEL)
    k_sh = bias_sh = vT = c_sh = k_max = None
    for l in range(DEPTH):
        if l == N_A:
            k_sh, bias_sh, vT, c_sh, kn = _kvf(h, kv_norm.reshape(1, D_MODEL), w_kv, w_f, b_f_row,
                                               batch, seq)
            k_max = jnp.sqrt(jnp.max(kn[:, 0, :FOX_HEADS].reshape(batch, -1, FOX_HEADS), axis=1))
            k_max = jnp.broadcast_to(k_max[:, :, None], (batch, FOX_HEADS, WIDE_TOKEN_TILE))
        h = _ffn(h, ffn_norm[l, 0].reshape(1, D_MODEL), wgu, wd, (l, 0), fg, False)
        g_mix = mix_norm[l].reshape(1, D_MODEL)
        mix = None
        if l < N_A:
            h = _mixer_a(h, g_mix, w_in, a_conv_w, w_out_a, mkT, mv, l, seq)
        else:
            j = l - N_A
            qT, y_mem, gap = _qproj(h, g_mix, w_q, mkT, mv, c_sh, k_max, l, batch, seq)
            y_tok = lax.cond(
                jnp.max(gap) < MAX_SHIFT_GAP,
                functools.partial(_fox_attention, online_max=False),
                functools.partial(_fox_attention, online_max=True),
                qT, k_sh, bias_sh, vT).reshape(tokens, FOX_WIDTH)
            mix = (y_tok, y_mem, w_out_b, j)
        h = _ffn(h, ffn_norm[l, 1].reshape(1, D_MODEL), wgu, wd, (l, 1), fg, l == DEPTH - 1, mix)
    return h.reshape(batch, seq, D_MODEL)
```

```python
import functools
import math

import jax
import jax.numpy as jnp
import numpy as np
from jax import lax
from jax.experimental import pallas as pl
from jax.experimental.pallas import tpu as pltpu

D_MODEL = 1024
DEPTH = 4
N_A = DEPTH // 2
MEM_LEN = 256
MEM_HEADS = 4
HEAD_DIM = 128
MEM_WIDTH = MEM_HEADS * HEAD_DIM
CONV_DIM = D_MODEL
CONV_WIDTH = 3
FOX_HEADS = 8
FOX_WIDTH = FOX_HEADS * HEAD_DIM
D_FF = 2816
RMS_EPS = 1e-6
NEG_INF = float(np.finfo(np.float32).min)
ATTN_SCALE = HEAD_DIM ** -0.5
LOG2E = math.log2(math.e)

V7X_LANES = 128
V7X_SUBLANES = 8
V7X_MXU_DIM = 256
V7X_VMEM_LIMIT_BYTES = 60000 * 1024

TOKEN_TILE = 512
WIDE_TOKEN_TILE = 1024
FF_CHUNK = V7X_MXU_DIM
ATTN_TILE = 512
ATTN_KEY_TILE = 256
ATTN_LOOKAHEAD = 1
ATTN_HEADS = 8
N_SPLIT = 3
V_PAD_ROWS = 16
V_ROWS = HEAD_DIM + V_PAD_ROWS
AUG_SHIFT = N_SPLIT * FOX_HEADS
AUG_ROWS = 32
assert AUG_SHIFT + N_SPLIT <= AUG_ROWS <= HEAD_DIM
BOUND_MARGIN = 1.02
MAX_SHIFT_GAP = 100.0

F32 = jnp.float32
BF16 = jnp.bfloat16


def _dot(a, b):
    return jnp.dot(a, b, preferred_element_type=F32)


def _rmsnorm(x, g):
    y = x * lax.rsqrt(jnp.mean(x * x, axis=-1, keepdims=True) + RMS_EPS)
    return y * g


def _split_bf16(x):
    pieces = []
    r = x
    for _ in range(N_SPLIT):
        p = r.astype(BF16)
        pieces.append(p)
        r = r - p.astype(F32)
    return pieces


def _head_sum_matrix():
    m = np.zeros((FOX_WIDTH, V7X_LANES), np.float32)
    for hh in range(FOX_HEADS):
        m[hh * HEAD_DIM:(hh + 1) * HEAD_DIM, hh] = 1.0
    return jnp.asarray(m, BF16)


def _params(n_grid_dims):
    return pltpu.CompilerParams(
        dimension_semantics=("arbitrary",) * n_grid_dims,
        vmem_limit_bytes=V7X_VMEM_LIMIT_BYTES)


def _resident(shape):
    zeros = (0,) * len(shape)
    return pl.BlockSpec(shape, lambda *_: zeros, pipeline_mode=pl.Buffered(1))


def _resident_layer(stacked, index):
    n = len(index)
    block_index = tuple(index) + (0,) * (stacked.ndim - n)
    return pl.BlockSpec((None,) * n + tuple(stacked.shape[n:]), lambda *_: block_index,
                        pipeline_mode=pl.Buffered(1))


def _mem_kv_kernel(mem_ref, g_ref, w_ref, mkT_ref, mv_ref, *, batch):
    h = _rmsnorm(mem_ref[...], g_ref[...]).astype(BF16)
    mkv = _dot(h, w_ref[0])
    for b in range(batch):
        blk = mkv[b * MEM_LEN:(b + 1) * MEM_LEN]
        mkT_ref[0, b] = blk[:, :MEM_WIDTH].T.astype(BF16)
        mv_ref[0, b] = blk[:, MEM_WIDTH:].astype(BF16)


def _mem_kv(mem2d, mem_norm, w_kv, batch):
    rows = mem2d.shape[0]
    return pl.pallas_call(
        functools.partial(_mem_kv_kernel, batch=batch),
        grid=(DEPTH,),
        in_specs=[
            pl.BlockSpec((rows, D_MODEL), lambda l: (0, 0)),
            pl.BlockSpec((1, D_MODEL), lambda l: (0, 0)),
            pl.BlockSpec((1, D_MODEL, 2 * MEM_WIDTH), lambda l: (l, 0, 0)),
        ],
        out_specs=[
            pl.BlockSpec((1, batch, MEM_WIDTH, MEM_LEN), lambda l: (l, 0, 0, 0)),
            pl.BlockSpec((1, batch, MEM_LEN, MEM_WIDTH), lambda l: (l, 0, 0, 0)),
        ],
        out_shape=[
            jax.ShapeDtypeStruct((DEPTH, batch, MEM_WIDTH, MEM_LEN), BF16),
            jax.ShapeDtypeStruct((DEPTH, batch, MEM_LEN, MEM_WIDTH), BF16),
        ],
        compiler_params=_params(1),
        name="mem_kv",
    )(mem2d, mem_norm, w_kv)


def _mem_attn(qm, mkT_ref, mv_ref, fillers=()):
    fillers = list(fillers)

    def logits(hh):
        sl = slice(hh * HEAD_DIM, (hh + 1) * HEAD_DIM)
        return _dot(qm[:, sl].astype(BF16), mkT_ref[0, 0, sl, :]) * ATTN_SCALE

    outs = []
    s_next = logits(0)
    for hh in range(MEM_HEADS):
        s = s_next
        if hh + 1 < MEM_HEADS:
            s_next = logits(hh + 1)
        if fillers:
            fillers.pop(0)()
        p = jnp.exp(s - jnp.max(s, axis=-1, keepdims=True))
        denom = jnp.sum(p, axis=-1, keepdims=True)
        outs.append(_dot(p.astype(BF16), mv_ref[0, 0, :, hh * HEAD_DIM:(hh + 1) * HEAD_DIM]) / denom)
    for f in fillers:
        f()
    return jnp.concatenate(outs, axis=-1).astype(BF16)


def _ffn_kernel(*refs, final, mixed):
    if mixed:
        x_ref, yt_ref, ym_ref, wout_ref, g_ref, wgu_ref, wd_ref, fg_ref, o_ref, act_ref = refs
        x = (x_ref[...] + _dot(yt_ref[...], wout_ref[:FOX_WIDTH, :])
             + _dot(ym_ref[...], wout_ref[FOX_WIDTH:, :]))
    else:
        x_ref, g_ref, wgu_ref, wd_ref, fg_ref, o_ref, act_ref = refs
        x = x_ref[...]
    h = _rmsnorm(x, g_ref[...]).astype(BF16)
    for c in range(D_FF // FF_CHUNK):
        lo = c * FF_CHUNK
        gate = _dot(h, wgu_ref[:, lo:lo + FF_CHUNK])
        up = _dot(h, wgu_ref[:, D_FF + lo:D_FF + lo + FF_CHUNK])
        act_ref[:, lo:lo + FF_CHUNK] = (gate * jax.nn.sigmoid(gate) * up).astype(BF16)
    y = x + 0.5 * _dot(act_ref[...], wd_ref[...])
    if final:
        y = _rmsnorm(y, fg_ref[...])
    o_ref[...] = y


def _ffn(x2d, g, wgu, wd, which, fg, final, mix=None):
    tokens = x2d.shape[0]
    tm = WIDE_TOKEN_TILE
    row = lambda i: (i, 0)
    weights = [_resident((1, D_MODEL)), _resident_layer(wgu, which),
               _resident_layer(wd, which), _resident((1, D_MODEL))]
    if mix:
        y_tok, y_mem, w_out, j = mix
        operands = (x2d, y_tok, y_mem, w_out, g, wgu, wd, fg)
        in_specs = [pl.BlockSpec((tm, D_MODEL), row), pl.BlockSpec((tm, FOX_WIDTH), row),
                    pl.BlockSpec((tm, MEM_WIDTH), row), _resident_layer(w_out, (j,))] + weights
    else:
        operands = (x2d, g, wgu, wd, fg)
        in_specs = [pl.BlockSpec((tm, D_MODEL), row)] + weights
    return pl.pallas_call(
        functools.partial(_ffn_kernel, final=final, mixed=bool(mix)),
        grid=(tokens // tm,),
        in_specs=in_specs,
        out_specs=pl.BlockSpec((tm, D_MODEL), row),
        out_shape=jax.ShapeDtypeStruct((tokens, D_MODEL), F32),
        scratch_shapes=[pltpu.VMEM((tm, D_FF), BF16)],
        compiler_params=_params(1),
        name=("ffn_mixed" if mix else "ffn") + ("_final" if final else ""),
    )(*operands)


def _mixer_a_kernel(x_ref, g_ref, win_ref, cw_ref, wout_ref, mkT_ref, mv_ref,
                    o_ref, tail_ref, *, tiles_per_seq):
    tm = x_ref.shape[0]
    c = CONV_DIM

    @pl.when(pl.program_id(0) % tiles_per_seq == 0)
    def _():
        tail_ref[...] = jnp.zeros_like(tail_ref)

    x = x_ref[...]
    h = _rmsnorm(x, g_ref[...]).astype(BF16)
    proj = {}

    def project(name, lo):
        def run():
            proj[name] = _dot(h, win_ref[:, lo:lo + c])
        return run

    y_mem = _mem_attn(_dot(h, win_ref[:, 3 * c:]), mkT_ref, mv_ref,
                      [project("gate_c", c), project("u", 2 * c), project("gate_b", 0)])
    v = proj["gate_c"] * proj["u"]

    prev = tail_ref[...]
    sub = lax.broadcasted_iota(jnp.int32, (V7X_SUBLANES, c), 0)
    shifted = []
    for k in range(1, CONV_WIDTH):
        rv = pltpu.roll(v, k, 0)
        head = jnp.where(sub < k, pltpu.roll(prev, k, 0), rv[:V7X_SUBLANES])
        shifted.append(jnp.concatenate([head, rv[V7X_SUBLANES:]], axis=0))
    tail_ref[...] = v[tm - V7X_SUBLANES:]

    cw = cw_ref[...]
    conv = cw[CONV_WIDTH - 1:CONV_WIDTH] * v
    for k in range(1, CONV_WIDTH):
        conv = conv + cw[CONV_WIDTH - 1 - k:CONV_WIDTH - k] * shifted[k - 1]
    y_tok = (proj["gate_b"] * conv).astype(BF16)
    o_ref[...] = x + _dot(y_tok, wout_ref[:c, :]) + _dot(y_mem, wout_ref[c:, :])


def _mixer_a(x2d, g, w_in, conv_w, w_out, mkT, mv, layer, seq):
    tokens = x2d.shape[0]
    tm = TOKEN_TILE
    tps = seq // tm
    row = lambda i: (i, 0)
    return pl.pallas_call(
        functools.partial(_mixer_a_kernel, tiles_per_seq=tps),
        grid=(tokens // tm,),
        in_specs=[
            pl.BlockSpec((tm, D_MODEL), row),
            _resident((1, D_MODEL)),
            _resident_layer(w_in, (layer,)),
            _resident_layer(conv_w, (layer,)),
            _resident_layer(w_out, (layer,)),
            pl.BlockSpec((1, 1, MEM_WIDTH, MEM_LEN), lambda i: (layer, i // tps, 0, 0)),
            pl.BlockSpec((1, 1, MEM_LEN, MEM_WIDTH), lambda i: (layer, i // tps, 0, 0)),
        ],
        out_specs=pl.BlockSpec((tm, D_MODEL), row),
        out_shape=jax.ShapeDtypeStruct((tokens, D_MODEL), F32),
        scratch_shapes=[pltpu.VMEM((V7X_SUBLANES, CONV_DIM), F32)],
        compiler_params=_params(1),
        name="mixer_a",
    )(x2d, g, w_in, conv_w, w_out, mkT, mv)


def _kvf_kernel(x_ref, g_ref, wkv_ref, wf_ref, bf_ref, tri_ref, sel_ref, hsum_ref,
                k_ref, bias_ref, vT_ref, c_ref, kn_ref, carry_ref, *, tiles_per_seq):
    tm = x_ref.shape[0]

    @pl.when(pl.program_id(0) % tiles_per_seq == 0)
    def _():
        carry_ref[...] = jnp.zeros_like(carry_ref)

    h = _rmsnorm(x_ref[...], g_ref[...]).astype(BF16)

    z = _dot(h, wf_ref[...]) + bf_ref[...]
    ls = -(jnp.maximum(-z, 0.0) + jnp.log1p(jnp.exp(-jnp.abs(z))))
    k = _dot(h, wkv_ref[:, :FOX_WIDTH])
    for hh in range(FOX_HEADS):
        k_ref[0, hh] = k[:, hh * HEAD_DIM:(hh + 1) * HEAD_DIM].astype(BF16)

    cs = _dot(tri_ref[...], jnp.concatenate(_split_bf16(ls), axis=1))
    cum = carry_ref[0:1, :]
    for p in range(N_SPLIT):
        cum = cum + cs[:, p * V7X_LANES:(p + 1) * V7X_LANES]
    carry_ref[...] = jnp.broadcast_to(cum[tm - 1:tm, :], carry_ref.shape)
    c_ref[0] = cum
    kn_ref[0] = jnp.broadcast_to(
        jnp.max(_dot((k * k).astype(BF16), hsum_ref[...]), axis=0, keepdims=True),
        kn_ref.shape[1:])
    v = _dot(h, wkv_ref[:, FOX_WIDTH:])

    bias = _dot(jnp.concatenate(_split_bf16(-LOG2E * cum), axis=1), sel_ref[...])
    lane = lax.broadcasted_iota(jnp.int32, (1, V7X_LANES), 1)
    ones_lanes = jnp.where((lane >= AUG_SHIFT) & (lane < AUG_SHIFT + N_SPLIT), 1.0, 0.0)
    bias_ref[0] = (bias + ones_lanes).astype(BF16)
    for hh in range(FOX_HEADS):
        sl = slice(hh * HEAD_DIM, (hh + 1) * HEAD_DIM)
        vT_ref[0, hh, :HEAD_DIM, :] = v[:, sl].T.astype(BF16)
    denom_rows = lax.broadcasted_iota(jnp.int32, (V_PAD_ROWS, tm), 0) == 0
    for hh in range(FOX_HEADS):
        vT_ref[0, hh, HEAD_DIM:, :] = jnp.where(denom_rows, 1.0, 0.0).astype(BF16)


def _kvf(x2d, g, w_kv, w_f, b_f, batch, seq):
    tokens = x2d.shape[0]
    tm = TOKEN_TILE
    tps = seq // tm
    tri = jnp.tril(jnp.ones((tm, tm), F32)).astype(BF16)
    sel = np.zeros((N_SPLIT * V7X_LANES, V7X_LANES), np.float32)
    for p in range(N_SPLIT):
        for hh in range(FOX_HEADS):
            sel[p * V7X_LANES + hh, N_SPLIT * hh + p] = 1.0
    sel = jnp.asarray(sel, BF16)
    hsum = _head_sum_matrix()
    return pl.pallas_call(
        functools.partial(_kvf_kernel, tiles_per_seq=tps),
        grid=(tokens // tm,),
        in_specs=[
            pl.BlockSpec((tm, D_MODEL), lambda i: (i, 0)),
            _resident((1, D_MODEL)),
            _resident(w_kv.shape),
            _resident(w_f.shape),
            _resident(b_f.shape),
            _resident(tri.shape),
            _resident(sel.shape),
            _resident(hsum.shape),
        ],
        out_specs=[
            pl.BlockSpec((1, FOX_HEADS, tm, HEAD_DIM), lambda i: (i // tps, 0, i % tps, 0)),
            pl.BlockSpec((1, tm, V7X_LANES), lambda i: (i // tps, i % tps, 0)),
            pl.BlockSpec((1, FOX_HEADS, V_ROWS, tm), lambda i: (i // tps, 0, 0, i % tps)),
            pl.BlockSpec((1, tm, V7X_LANES), lambda i: (i // tps, i % tps, 0)),
            pl.BlockSpec((1, V7X_SUBLANES, V7X_LANES), lambda i: (i, 0, 0)),
        ],
        out_shape=[
            jax.ShapeDtypeStruct((batch, FOX_HEADS, seq, HEAD_DIM), BF16),
            jax.ShapeDtypeStruct((batch, seq, V7X_LANES), BF16),
            jax.ShapeDtypeStruct((batch, FOX_HEADS, V_ROWS, seq), BF16),
            jax.ShapeDtypeStruct((batch, seq, V7X_LANES), F32),
            jax.ShapeDtypeStruct((tokens // tm, V7X_SUBLANES, V7X_LANES), F32),
        ],
        scratch_shapes=[pltpu.VMEM((V7X_SUBLANES, V7X_LANES), F32)],
        compiler_params=_params(1),
        name="kvf",
    )(x2d, g, w_kv, w_f, b_f, tri, sel, hsum)


def _qproj_kernel(x_ref, g_ref, wq_ref, mkT_ref, mv_ref, c_ref, kmax_ref,
                  qT_ref, ymem_ref, gap_ref):
    tm = x_ref.shape[0]
    h = _rmsnorm(x_ref[...], g_ref[...]).astype(BF16)
    cT = c_ref[0].T
    row = lax.broadcasted_iota(jnp.int32, (AUG_ROWS, tm), 0)
    gaps = []

    def finish_head(hh, q):
        qT = q.T
        reach =(BOUND_MARGIN * jnp.sqrt(jnp.sum(qT * qT, axis=0, keepdims=True))
                 * kmax_ref[0, hh:hh + 1, :])
        gaps.append(2.0 * reach)
        aug = jnp.where((row >= N_SPLIT * hh) & (row < N_SPLIT * (hh + 1)), 1.0, 0.0)
        for p, piece in enumerate(_split_bf16(LOG2E * cT[hh:hh + 1, :] - reach)):
            aug = jnp.where(row == AUG_SHIFT + p, piece.astype(F32), aug)
        qT_ref[0, hh, :HEAD_DIM, :] = qT.astype(BF16)
        qT_ref[0, hh, HEAD_DIM:HEAD_DIM + AUG_ROWS, :] = aug.astype(BF16)
        qT_ref[0, hh, HEAD_DIM + AUG_ROWS:, :] = jnp.zeros((HEAD_DIM - AUG_ROWS, tm), BF16)

    def project_pair(first):
        def run():
            q2 = (_dot(h, wq_ref[:, first * HEAD_DIM:(first + 2) * HEAD_DIM])
                  * (ATTN_SCALE * LOG2E))
            finish_head(first, q2[:, :HEAD_DIM])
            finish_head(first + 1, q2[:, HEAD_DIM:])
        return run

    ymem_ref[...] = _mem_attn(_dot(h, wq_ref[:, FOX_WIDTH:]), mkT_ref, mv_ref,
                              [project_pair(hh) for hh in range(0, FOX_HEADS, 2)])
    gap_ref[0] = jnp.broadcast_to(functools.reduce(jnp.maximum, gaps), gap_ref.shape[1:])


def _qproj(x2d, g, w_q, mkT, mv, c, kmax, layer, batch, seq):
    tokens = x2d.shape[0]
    tm = WIDE_TOKEN_TILE
    tps = seq // tm
    return pl.pallas_call(
        _qproj_kernel,
        grid=(tokens // tm,),
        in_specs=[
            pl.BlockSpec((tm, D_MODEL), lambda i: (i, 0)),
            _resident((1, D_MODEL)),
            _resident_layer(w_q, (layer - N_A,)),
            pl.BlockSpec((1, 1, MEM_WIDTH, MEM_LEN), lambda i: (layer, i // tps, 0, 0)),
            pl.BlockSpec((1, 1, MEM_LEN, MEM_WIDTH), lambda i: (layer, i // tps, 0, 0)),
            pl.BlockSpec((1, tm, V7X_LANES), lambda i: (i // tps, i % tps, 0)),
            pl.BlockSpec((1, FOX_HEADS, tm), lambda i: (i // tps, 0, 0)),
        ],
        out_specs=[
            pl.BlockSpec((1, FOX_HEADS, 2 * HEAD_DIM, tm), lambda i: (i // tps, 0, 0, i % tps)),
            pl.BlockSpec((tm, MEM_WIDTH), lambda i: (i, 0)),
            pl.BlockSpec((1, V7X_SUBLANES, tm), lambda i: (i, 0, 0)),
        ],
        out_shape=[
            jax.ShapeDtypeStruct((batch, FOX_HEADS, 2 * HEAD_DIM, seq), BF16),
            jax.ShapeDtypeStruct((tokens, MEM_WIDTH), BF16),
            jax.ShapeDtypeStruct((tokens // tm, V7X_SUBLANES, tm), F32),
        ],
        compiler_params=_params(1),
        name="qproj",
    )(x2d, g, w_q, mkT, mv, c, kmax)


def _fox_kernel(qT_ref, k_hbm, bias_hbm, vT_hbm, o_ref, m_ref, acc_ref, s_ref,
                k_ref, bias_ref, vT_ref, kv_sem, *, online_max):
    t = ATTN_TILE
    tk = ATTN_KEY_TILE
    heads = qT_ref.shape[1]
    b, hg, qi = pl.program_id(0), pl.program_id(1), pl.program_id(2)
    q0 = pl.multiple_of(qi * t, t)

    def key_chunk_copies(chunk):
        rows = pl.ds(pl.multiple_of(chunk * t, t), t)
        heads_of_group = pl.ds(hg * heads, heads)
        slot = chunk % 2
        return (
            pltpu.make_async_copy(k_hbm.at[b, heads_of_group, rows, :],
                                  k_ref.at[:, rows, :], kv_sem.at[0, slot]),
            pltpu.make_async_copy(bias_hbm.at[b, rows, :], bias_ref.at[rows, :], kv_sem.at[1, slot]),
            pltpu.make_async_copy(vT_hbm.at[b, heads_of_group, :, rows],
                                  vT_ref.at[:, :, rows], kv_sem.at[2, slot]),
        )

    @pl.when(qi == 0)
    def _():
        for copy in key_chunk_copies(0):
            copy.start()

    @pl.when(qi + 1 < pl.num_programs(2))
    def _():
        for copy in key_chunk_copies(qi + 1):
            copy.start()

    for copy in key_chunk_copies(qi):
        copy.wait()

    if online_max:
        m_ref[...] = jnp.full_like(m_ref, -jnp.inf)
    acc_ref[...] = jnp.zeros_like(acc_ref)

    def logits(unit):
        g, k0, diag_offset = unit
        ka = jnp.concatenate([k_ref[g, pl.ds(k0, tk), :], bias_ref[pl.ds(k0, tk), :]],
                             axis=1)
        return _dot(ka, qT_ref[0, g, :, (diag_offset or 0):])

    def softmax_step(unit, s):
        g, k0, diag_offset = unit
        lo = diag_offset or 0
        if diag_offset is not None:
            causal = (lax.broadcasted_iota(jnp.int32, s.shape, 0)
                      <= lax.broadcasted_iota(jnp.int32, s.shape, 1))
            s = jnp.where(causal, s, NEG_INF)
        vt = vT_ref[g, :, pl.ds(k0, tk)]
        if online_max:
            m_prev = m_ref[g, :, lo:]
            m_new = jnp.maximum(m_prev, jnp.max(s, axis=0, keepdims=True))
            alpha = jnp.exp2(m_prev - m_new)
            acc_ref[g, :, lo:] = (alpha * acc_ref[g, :, lo:]
                                  + _dot(vt, jnp.exp2(s - m_new).astype(BF16)))
            m_ref[g, :, lo:] = m_new
        else:
            acc_ref[g, :, lo:] += _dot(vt, jnp.exp2(s).astype(BF16))

    def first_units(k0):
        return [(g, k0, None) for g in range(ATTN_LOOKAHEAD)]

    def run(units, following):
        pending = [s_ref[j] for j in range(ATTN_LOOKAHEAD)]
        seq = units + following
        for i, u in enumerate(units):
            s = pending.pop(0)
            if i + ATTN_LOOKAHEAD < len(seq):
                pending.append(logits(seq[i + ATTN_LOOKAHEAD]))
            softmax_step(u, s)
        for j, s in enumerate(pending):
            s_ref[j] = s

    def full_keys(base, steps):
        run([(g, pl.multiple_of(base + j * tk, tk), None)
             for j in range(steps) for g in range(heads)],
            first_units(pl.multiple_of(base + steps * tk, tk)))

    tile_steps = t // tk

    def body(ki, c):
        full_keys(ki * (2 * t), 2 * tile_steps)
        return c

    for j, u in enumerate(first_units(0)):
        s_ref[j] = logits(u)
    lax.fori_loop(0, qi // 2, body, 0)

    @pl.when(qi % 2 == 1)
    def _():
        full_keys((qi - 1) * t, tile_steps)

    run([(g, pl.multiple_of(q0 + d * tk, tk), d * tk)
         for d in range(t // tk) for g in range(heads)], [])
    for g in range(heads):
        out = acc_ref[g, :HEAD_DIM, :] / acc_ref[g, HEAD_DIM:HEAD_DIM + 1, :]
        o_ref[0, :, g * HEAD_DIM:(g + 1) * HEAD_DIM] = out.T.astype(BF16)


def _fox_attention(qT, k, bias, vT, online_max):
    batch, heads, _, seq = qT.shape
    t = ATTN_TILE
    hg = ATTN_HEADS
    assert ATTN_LOOKAHEAD <= hg and t % ATTN_KEY_TILE == 0
    hbm = pl.BlockSpec(memory_space=pl.ANY)
    return pl.pallas_call(
        functools.partial(_fox_kernel, online_max=online_max),
        grid=(batch, heads // hg, seq // t),
        in_specs=[
            pl.BlockSpec((1, hg, 2 * HEAD_DIM, t), lambda b, h, q: (b, h, 0, q)),
            hbm, hbm, hbm,
        ],
        out_specs=pl.BlockSpec((1, t, hg * HEAD_DIM), lambda b, h, q: (b, q, h)),
        out_shape=jax.ShapeDtypeStruct((batch, seq, heads * HEAD_DIM), BF16),
        scratch_shapes=[
            pltpu.VMEM((hg, 1, t), F32),
            pltpu.VMEM((hg, V_ROWS, t), F32),
            pltpu.VMEM((ATTN_LOOKAHEAD, ATTN_KEY_TILE, t), F32),
            pltpu.VMEM((hg, seq, HEAD_DIM), BF16),
            pltpu.VMEM((seq, V7X_LANES), BF16),
            pltpu.VMEM((hg, V_ROWS, seq), BF16),
            pltpu.SemaphoreType.DMA((3, 2)),
        ],
        compiler_params=_params(3),
        name="fox_attention_online" if online_max else "fox_attention_bounded",
    )(qT, k, bias, vT)


def kernel(x, mem, ffn_norm, ffn_w_gate_up, ffn_w_down, mix_norm, mem_norm, mem_w_kv,
           a_w_in, a_conv_w, a_w_out, kv_norm, w_kvf, b_f, b_w_q, b_w_out, final_norm):
    batch, seq, d = x.shape
    assert d == D_MODEL and seq % TOKEN_TILE == 0 and seq % ATTN_TILE == 0
    assert mem.shape == (batch, MEM_LEN, D_MODEL)
    tokens = batch * seq

    wgu = ffn_w_gate_up.astype(BF16)
    wd = ffn_w_down.astype(BF16)
    w_mem = mem_w_kv.astype(BF16)
    w_in = a_w_in.astype(BF16)
    w_out_a = a_w_out.astype(BF16)
    w_kv = w_kvf[:, :2 * FOX_WIDTH].astype(BF16)
    w_f = jnp.pad(w_kvf[:, 2 * FOX_WIDTH:], ((0, 0), (0, V7X_LANES - FOX_HEADS))).astype(BF16)
    b_f_row = jnp.pad(b_f, (0, V7X_LANES - FOX_HEADS)).reshape(1, V7X_LANES)
    w_q = b_w_q.astype(BF16)
    w_out_b = b_w_out.astype(BF16)
    fg = final_norm.reshape(1, D_MODEL)

    mkT, mv = _mem_kv(mem.reshape(batch * MEM_LEN, D_MODEL), mem_norm.reshape(1, D_MODEL),
                      w_mem, batch)

    h = x.reshape(tokens, D_MODEL)
    k_sh = bias_sh = vT = c_sh = k_max = None
    for l in range(DEPTH):
        if l == N_A:
            k_sh, bias_sh, vT, c_sh, kn = _kvf(h, kv_norm.reshape(1, D_MODEL), w_kv, w_f, b_f_row,
                                               batch, seq)
            k_max = jnp.sqrt(jnp.max(kn[:, 0, :FOX_HEADS].reshape(batch, -1, FOX_HEADS), axis=1))
            k_max = jnp.broadcast_to(k_max[:, :, None], (batch, FOX_HEADS, WIDE_TOKEN_TILE))
        h = _ffn(h, ffn_norm[l, 0].reshape(1, D_MODEL), wgu, wd, (l, 0), fg, False)
        g_mix = mix_norm[l].reshape(1, D_MODEL)
        mix = None
        if l < N_A:
            h = _mixer_a(h, g_mix, w_in, a_conv_w, w_out_a, mkT, mv, l, seq)
        else:
            j = l - N_A
            qT, y_mem, gap = _qproj(h, g_mix, w_q, mkT, mv, c_sh, k_max, l, batch, seq)
            y_tok = lax.cond(
                jnp.max(gap) < MAX_SHIFT_GAP,
                functools.partial(_fox_attention, online_max=False),
                functools.partial(_fox_attention, online_max=True),
                qT, k_sh, bias_sh, vT).reshape(tokens, FOX_WIDTH)
            mix = (y_tok, y_mem, w_out_b, j)
        h = _ffn(h, ffn_norm[l, 1].reshape(1, D_MODEL), wgu, wd, (l, 1), fg, l == DEPTH - 1, mix)
    return h.reshape(batch, seq, D_MODEL)
```

```python
import functools
import math

import jax
import jax.numpy as jnp
import numpy as np
from jax import lax
from jax.experimental import pallas as pl
from jax.experimental.pallas import tpu as pltpu

D_MODEL = 1024
DEPTH = 4
N_A = DEPTH // 2
MEM_LEN = 256
MEM_HEADS = 4
HEAD_DIM = 128
MEM_WIDTH = MEM_HEADS * HEAD_DIM
CONV_DIM = D_MODEL
CONV_WIDTH = 3
FOX_HEADS = 8
FOX_WIDTH = FOX_HEADS * HEAD_DIM
D_FF = 2816
RMS_EPS = 1e-6
NEG_INF = float(np.finfo(np.float32).min)
ATTN_SCALE = HEAD_DIM ** -0.5
LOG2E = math.log2(math.e)

V7X_LANES = 128
V7X_SUBLANES = 8
V7X_MXU_DIM = 256
V7X_VMEM_LIMIT_BYTES = 60000 * 1024

TOKEN_TILE = 512
WIDE_TOKEN_TILE = 1024
FF_CHUNK = V7X_MXU_DIM
ATTN_TILE = 512
ATTN_KEY_TILE = 256
ATTN_LOOKAHEAD = 3
ATTN_HEADS = 8
N_SPLIT = 3
V_PAD_ROWS = 16
V_ROWS = HEAD_DIM + V_PAD_ROWS
AUG_SHIFT = N_SPLIT * FOX_HEADS
AUG_ROWS = 32
assert AUG_SHIFT + N_SPLIT <= AUG_ROWS <= HEAD_DIM
BOUND_MARGIN = 1.02
MAX_SHIFT_GAP = 100.0

F32 = jnp.float32
BF16 = jnp.bfloat16


def _dot(a, b):
    return jnp.dot(a, b, preferred_element_type=F32)


def _rmsnorm(x, g):
    y = x * lax.rsqrt(jnp.mean(x * x, axis=-1, keepdims=True) + RMS_EPS)
    return y * g


def _split_bf16(x):
    pieces = []
    r = x
    for _ in range(N_SPLIT):
        p = r.astype(BF16)
        pieces.append(p)
        r = r - p.astype(F32)
    return pieces


def _head_sum_matrix():
    m = np.zeros((FOX_WIDTH, V7X_LANES), np.float32)
    for hh in range(FOX_HEADS):
        m[hh * HEAD_DIM:(hh + 1) * HEAD_DIM, hh] = 1.0
    return jnp.asarray(m, BF16)


def _params(n_grid_dims):
    return pltpu.CompilerParams(
        dimension_semantics=("arbitrary",) * n_grid_dims,
        vmem_limit_bytes=V7X_VMEM_LIMIT_BYTES)


def _resident(shape):
    zeros = (0,) * len(shape)
    return pl.BlockSpec(shape, lambda *_: zeros, pipeline_mode=pl.Buffered(1))


def _resident_layer(stacked, index):
    n = len(index)
    block_index = tuple(index) + (0,) * (stacked.ndim - n)
    return pl.BlockSpec((None,) * n + tuple(stacked.shape[n:]), lambda *_: block_index,
                        pipeline_mode=pl.Buffered(1))


def _mem_kv_kernel(mem_ref, g_ref, w_ref, mkT_ref, mv_ref, *, batch):
    h = _rmsnorm(mem_ref[...], g_ref[...]).astype(BF16)
    mkv = _dot(h, w_ref[0])
    for b in range(batch):
        blk = mkv[b * MEM_LEN:(b + 1) * MEM_LEN]
        mkT_ref[0, b] = blk[:, :MEM_WIDTH].T.astype(BF16)
        mv_ref[0, b] = blk[:, MEM_WIDTH:].astype(BF16)


def _mem_kv(mem2d, mem_norm, w_kv, batch):
    rows = mem2d.shape[0]
    return pl.pallas_call(
        functools.partial(_mem_kv_kernel, batch=batch),
        grid=(DEPTH,),
        in_specs=[
            pl.BlockSpec((rows, D_MODEL), lambda l: (0, 0)),
            pl.BlockSpec((1, D_MODEL), lambda l: (0, 0)),
            pl.BlockSpec((1, D_MODEL, 2 * MEM_WIDTH), lambda l: (l, 0, 0)),
        ],
        out_specs=[
            pl.BlockSpec((1, batch, MEM_WIDTH, MEM_LEN), lambda l: (l, 0, 0, 0)),
            pl.BlockSpec((1, batch, MEM_LEN, MEM_WIDTH), lambda l: (l, 0, 0, 0)),
        ],
        out_shape=[
            jax.ShapeDtypeStruct((DEPTH, batch, MEM_WIDTH, MEM_LEN), BF16),
            jax.ShapeDtypeStruct((DEPTH, batch, MEM_LEN, MEM_WIDTH), BF16),
        ],
        compiler_params=_params(1),
        name="mem_kv",
    )(mem2d, mem_norm, w_kv)


def _mem_attn(qm, mkT_ref, mv_ref, fillers=()):
    fillers = list(fillers)

    def logits(hh):
        sl = slice(hh * HEAD_DIM, (hh + 1) * HEAD_DIM)
        return _dot(qm[:, sl].astype(BF16), mkT_ref[0, 0, sl, :]) * ATTN_SCALE

    outs = []
    s_next = logits(0)
    for hh in range(MEM_HEADS):
        s = s_next
        if hh + 1 < MEM_HEADS:
            s_next = logits(hh + 1)
        if fillers:
            fillers.pop(0)()
        p = jnp.exp(s - jnp.max(s, axis=-1, keepdims=True))
        denom = jnp.sum(p, axis=-1, keepdims=True)
        outs.append(_dot(p.astype(BF16), mv_ref[0, 0, :, hh * HEAD_DIM:(hh + 1) * HEAD_DIM]) / denom)
    for f in fillers:
        f()
    return jnp.concatenate(outs, axis=-1).astype(BF16)


def _ffn_kernel(*refs, final, mixed):
    if mixed:
        x_ref, yt_ref, ym_ref, wout_ref, g_ref, wgu_ref, wd_ref, fg_ref, o_ref, act_ref = refs
        x = (x_ref[...] + _dot(yt_ref[...], wout_ref[:FOX_WIDTH, :])
             + _dot(ym_ref[...], wout_ref[FOX_WIDTH:, :]))
    else:
        x_ref, g_ref, wgu_ref, wd_ref, fg_ref, o_ref, act_ref = refs
        x = x_ref[...]
    h = _rmsnorm(x, g_ref[...]).astype(BF16)
    for c in range(D_FF // FF_CHUNK):
        lo = c * FF_CHUNK
        gate = _dot(h, wgu_ref[:, lo:lo + FF_CHUNK])
        up = _dot(h, wgu_ref[:, D_FF + lo:D_FF + lo + FF_CHUNK])
        act_ref[:, lo:lo + FF_CHUNK] = (gate * jax.nn.sigmoid(gate) * up).astype(BF16)
    y = x + 0.5 * _dot(act_ref[...], wd_ref[...])
    if final:
        y = _rmsnorm(y, fg_ref[...])
    o_ref[...] = y


def _ffn(x2d, g, wgu, wd, which, fg, final, mix=None):
    tokens = x2d.shape[0]
    tm = WIDE_TOKEN_TILE
    row = lambda i: (i, 0)
    weights = [_resident((1, D_MODEL)), _resident_layer(wgu, which),
               _resident_layer(wd, which), _resident((1, D_MODEL))]
    if mix:
        y_tok, y_mem, w_out, j = mix
        operands = (x2d, y_tok, y_mem, w_out, g, wgu, wd, fg)
        in_specs = [pl.BlockSpec((tm, D_MODEL), row), pl.BlockSpec((tm, FOX_WIDTH), row),
                    pl.BlockSpec((tm, MEM_WIDTH), row), _resident_layer(w_out, (j,))] + weights
    else:
        operands = (x2d, g, wgu, wd, fg)
        in_specs = [pl.BlockSpec((tm, D_MODEL), row)] + weights
    return pl.pallas_call(
        functools.partial(_ffn_kernel, final=final, mixed=bool(mix)),
        grid=(tokens // tm,),
        in_specs=in_specs,
        out_specs=pl.BlockSpec((tm, D_MODEL), row),
        out_shape=jax.ShapeDtypeStruct((tokens, D_MODEL), F32),
        scratch_shapes=[pltpu.VMEM((tm, D_FF), BF16)],
        compiler_params=_params(1),
        name=("ffn_mixed" if mix else "ffn") + ("_final" if final else ""),
    )(*operands)


def _mixer_a_kernel(x_ref, g_ref, win_ref, cw_ref, wout_ref, mkT_ref, mv_ref,
                    o_ref, tail_ref, *, tiles_per_seq):
    tm = x_ref.shape[0]
    c = CONV_DIM

    @pl.when(pl.program_id(0) % tiles_per_seq == 0)
    def _():
        tail_ref[...] = jnp.zeros_like(tail_ref)

    x = x_ref[...]
    h = _rmsnorm(x, g_ref[...]).astype(BF16)
    proj = {}

    def project(name, lo):
        def run():
            proj[name] = _dot(h, win_ref[:, lo:lo + c])
        return run

    y_mem = _mem_attn(_dot(h, win_ref[:, 3 * c:]), mkT_ref, mv_ref,
                      [project("gate_c", c), project("u", 2 * c), project("gate_b", 0)])
    v = proj["gate_c"] * proj["u"]

    prev = tail_ref[...]
    sub = lax.broadcasted_iota(jnp.int32, (V7X_SUBLANES, c), 0)
    shifted = []
    for k in range(1, CONV_WIDTH):
        rv = pltpu.roll(v, k, 0)
        head = jnp.where(sub < k, pltpu.roll(prev, k, 0), rv[:V7X_SUBLANES])
        shifted.append(jnp.concatenate([head, rv[V7X_SUBLANES:]], axis=0))
    tail_ref[...] = v[tm - V7X_SUBLANES:]

    cw = cw_ref[...]
    conv = cw[CONV_WIDTH - 1:CONV_WIDTH] * v
    for k in range(1, CONV_WIDTH):
        conv = conv + cw[CONV_WIDTH - 1 - k:CONV_WIDTH - k] * shifted[k - 1]
    y_tok = (proj["gate_b"] * conv).astype(BF16)
    o_ref[...] = x + _dot(y_tok, wout_ref[:c, :]) + _dot(y_mem, wout_ref[c:, :])


def _mixer_a(x2d, g, w_in, conv_w, w_out, mkT, mv, layer, seq):
    tokens = x2d.shape[0]
    tm = TOKEN_TILE
    tps = seq // tm
    row = lambda i: (i, 0)
    return pl.pallas_call(
        functools.partial(_mixer_a_kernel, tiles_per_seq=tps),
        grid=(tokens // tm,),
        in_specs=[
            pl.BlockSpec((tm, D_MODEL), row),
            _resident((1, D_MODEL)),
            _resident_layer(w_in, (layer,)),
            _resident_layer(conv_w, (layer,)),
            _resident_layer(w_out, (layer,)),
            pl.BlockSpec((1, 1, MEM_WIDTH, MEM_LEN), lambda i: (layer, i // tps, 0, 0)),
            pl.BlockSpec((1, 1, MEM_LEN, MEM_WIDTH), lambda i: (layer, i // tps, 0, 0)),
        ],
        out_specs=pl.BlockSpec((tm, D_MODEL), row),
        out_shape=jax.ShapeDtypeStruct((tokens, D_MODEL), F32),
        scratch_shapes=[pltpu.VMEM((V7X_SUBLANES, CONV_DIM), F32)],
        compiler_params=_params(1),
        name="mixer_a",
    )(x2d, g, w_in, conv_w, w_out, mkT, mv)


def _kvf_kernel(x_ref, g_ref, wkv_ref, wf_ref, bf_ref, tri_ref, sel_ref, hsum_ref,
                k_ref, bias_ref, vT_ref, c_ref, kn_ref, carry_ref, *, tiles_per_seq):
    tm = x_ref.shape[0]

    @pl.when(pl.program_id(0) % tiles_per_seq == 0)
    def _():
        carry_ref[...] = jnp.zeros_like(carry_ref)

    h = _rmsnorm(x_ref[...], g_ref[...]).astype(BF16)

    z = _dot(h, wf_ref[...]) + bf_ref[...]
    ls = -(jnp.maximum(-z, 0.0) + jnp.log1p(jnp.exp(-jnp.abs(z))))
    k = _dot(h, wkv_ref[:, :FOX_WIDTH])
    for hh in range(FOX_HEADS):
        k_ref[0, hh] = k[:, hh * HEAD_DIM:(hh + 1) * HEAD_DIM].astype(BF16)

    cs = _dot(tri_ref[...], jnp.concatenate(_split_bf16(ls), axis=1))
    cum = carry_ref[0:1, :]
    for p in range(N_SPLIT):
        cum = cum + cs[:, p * V7X_LANES:(p + 1) * V7X_LANES]
    carry_ref[...] = jnp.broadcast_to(cum[tm - 1:tm, :], carry_ref.shape)
    c_ref[0] = cum
    kn_ref[0] = jnp.broadcast_to(
        jnp.max(_dot((k * k).astype(BF16), hsum_ref[...]), axis=0, keepdims=True),
        kn_ref.shape[1:])
    v = _dot(h, wkv_ref[:, FOX_WIDTH:])

    bias = _dot(jnp.concatenate(_split_bf16(-LOG2E * cum), axis=1), sel_ref[...])
    lane = lax.broadcasted_iota(jnp.int32, (1, V7X_LANES), 1)
    ones_lanes = jnp.where((lane >= AUG_SHIFT) & (lane < AUG_SHIFT + N_SPLIT), 1.0, 0.0)
    bias_ref[0] = (bias + ones_lanes).astype(BF16)
    for hh in range(FOX_HEADS):
        sl = slice(hh * HEAD_DIM, (hh + 1) * HEAD_DIM)
        vT_ref[0, hh, :HEAD_DIM, :] = v[:, sl].T.astype(BF16)
    denom_rows = lax.broadcasted_iota(jnp.int32, (V_PAD_ROWS, tm), 0) == 0
    for hh in range(FOX_HEADS):
        vT_ref[0, hh, HEAD_DIM:, :] = jnp.where(denom_rows, 1.0, 0.0).astype(BF16)


def _kvf(x2d, g, w_kv, w_f, b_f, batch, seq):
    tokens = x2d.shape[0]
    tm = TOKEN_TILE
    tps = seq // tm
    tri = jnp.tril(jnp.ones((tm, tm), F32)).astype(BF16)
    sel = np.zeros((N_SPLIT * V7X_LANES, V7X_LANES), np.float32)
    for p in range(N_SPLIT):
        for hh in range(FOX_HEADS):
            sel[p * V7X_LANES + hh, N_SPLIT * hh + p] = 1.0
    sel = jnp.asarray(sel, BF16)
    hsum = _head_sum_matrix()
    return pl.pallas_call(
        functools.partial(_kvf_kernel, tiles_per_seq=tps),
        grid=(tokens // tm,),
        in_specs=[
            pl.BlockSpec((tm, D_MODEL), lambda i: (i, 0)),
            _resident((1, D_MODEL)),
            _resident(w_kv.shape),
            _resident(w_f.shape),
            _resident(b_f.shape),
            _resident(tri.shape),
            _resident(sel.shape),
            _resident(hsum.shape),
        ],
        out_specs=[
            pl.BlockSpec((1, FOX_HEADS, tm, HEAD_DIM), lambda i: (i // tps, 0, i % tps, 0)),
            pl.BlockSpec((1, tm, V7X_LANES), lambda i: (i // tps, i % tps, 0)),
            pl.BlockSpec((1, FOX_HEADS, V_ROWS, tm), lambda i: (i // tps, 0, 0, i % tps)),
            pl.BlockSpec((1, tm, V7X_LANES), lambda i: (i // tps, i % tps, 0)),
            pl.BlockSpec((1, V7X_SUBLANES, V7X_LANES), lambda i: (i, 0, 0)),
        ],
        out_shape=[
            jax.ShapeDtypeStruct((batch, FOX_HEADS, seq, HEAD_DIM), BF16),
            jax.ShapeDtypeStruct((batch, seq, V7X_LANES), BF16),
            jax.ShapeDtypeStruct((batch, FOX_HEADS, V_ROWS, seq), BF16),
            jax.ShapeDtypeStruct((batch, seq, V7X_LANES), F32),
            jax.ShapeDtypeStruct((tokens // tm, V7X_SUBLANES, V7X_LANES), F32),
        ],
        scratch_shapes=[pltpu.VMEM((V7X_SUBLANES, V7X_LANES), F32)],
        compiler_params=_params(1),
        name="kvf",
    )(x2d, g, w_kv, w_f, b_f, tri, sel, hsum)


def _qproj_kernel(x_ref, g_ref, wq_ref, mkT_ref, mv_ref, c_ref, kmax_ref,
                  qT_ref, ymem_ref, gap_ref):
    tm = x_ref.shape[0]
    h = _rmsnorm(x_ref[...], g_ref[...]).astype(BF16)
    cT = c_ref[0].T
    row = lax.broadcasted_iota(jnp.int32, (AUG_ROWS, tm), 0)
    gaps = []

    def finish_head(hh, q):
        qT = q.T
        reach =(BOUND_MARGIN * jnp.sqrt(jnp.sum(qT * qT, axis=0, keepdims=True))
                 * kmax_ref[0, hh:hh + 1, :])
        gaps.append(2.0 * reach)
        aug = jnp.where((row >= N_SPLIT * hh) & (row < N_SPLIT * (hh + 1)), 1.0, 0.0)
        for p, piece in enumerate(_split_bf16(LOG2E * cT[hh:hh + 1, :] - reach)):
            aug = jnp.where(row == AUG_SHIFT + p, piece.astype(F32), aug)
        qT_ref[0, hh, :HEAD_DIM, :] = qT.astype(BF16)
        qT_ref[0, hh, HEAD_DIM:HEAD_DIM + AUG_ROWS, :] = aug.astype(BF16)
        qT_ref[0, hh, HEAD_DIM + AUG_ROWS:, :] = jnp.zeros((HEAD_DIM - AUG_ROWS, tm), BF16)

    def project_pair(first):
        def run():
            q2 = (_dot(h, wq_ref[:, first * HEAD_DIM:(first + 2) * HEAD_DIM])
                  * (ATTN_SCALE * LOG2E))
            finish_head(first, q2[:, :HEAD_DIM])
            finish_head(first + 1, q2[:, HEAD_DIM:])
        return run

    ymem_ref[...] = _mem_attn(_dot(h, wq_ref[:, FOX_WIDTH:]), mkT_ref, mv_ref,
                              [project_pair(hh) for hh in range(0, FOX_HEADS, 2)])
    gap_ref[0] = jnp.broadcast_to(functools.reduce(jnp.maximum, gaps), gap_ref.shape[1:])


def _qproj(x2d, g, w_q, mkT, mv, c, kmax, layer, batch, seq):
    tokens = x2d.shape[0]
    tm = WIDE_TOKEN_TILE
    tps = seq // tm
    return pl.pallas_call(
        _qproj_kernel,
        grid=(tokens // tm,),
        in_specs=[
            pl.BlockSpec((tm, D_MODEL), lambda i: (i, 0)),
            _resident((1, D_MODEL)),
            _resident_layer(w_q, (layer - N_A,)),
            pl.BlockSpec((1, 1, MEM_WIDTH, MEM_LEN), lambda i: (layer, i // tps, 0, 0)),
            pl.BlockSpec((1, 1, MEM_LEN, MEM_WIDTH), lambda i: (layer, i // tps, 0, 0)),
            pl.BlockSpec((1, tm, V7X_LANES), lambda i: (i // tps, i % tps, 0)),
            pl.BlockSpec((1, FOX_HEADS, tm), lambda i: (i // tps, 0, 0)),
        ],
        out_specs=[
            pl.BlockSpec((1, FOX_HEADS, 2 * HEAD_DIM, tm), lambda i: (i // tps, 0, 0, i % tps)),
            pl.BlockSpec((tm, MEM_WIDTH), lambda i: (i, 0)),
            pl.BlockSpec((1, V7X_SUBLANES, tm), lambda i: (i, 0, 0)),
        ],
        out_shape=[
            jax.ShapeDtypeStruct((batch, FOX_HEADS, 2 * HEAD_DIM, seq), BF16),
            jax.ShapeDtypeStruct((tokens, MEM_WIDTH), BF16),
            jax.ShapeDtypeStruct((tokens // tm, V7X_SUBLANES, tm), F32),
        ],
        compiler_params=_params(1),
        name="qproj",
    )(x2d, g, w_q, mkT, mv, c, kmax)


def _fox_kernel(qT_ref, k_hbm, bias_hbm, vT_hbm, o_ref, m_ref, acc_ref, s_ref,
                k_ref, bias_ref, vT_ref, kv_sem, *, online_max):
    t = ATTN_TILE
    tk = ATTN_KEY_TILE
    heads = qT_ref.shape[1]
    b, hg, qi = pl.program_id(0), pl.program_id(1), pl.program_id(2)
    q0 = pl.multiple_of(qi * t, t)

    def key_chunk_copies(chunk):
        rows = pl.ds(pl.multiple_of(chunk * t, t), t)
        heads_of_group = pl.ds(hg * heads, heads)
        slot = chunk % 2
        return (
            pltpu.make_async_copy(k_hbm.at[b, heads_of_group, rows, :],
                                  k_ref.at[:, rows, :], kv_sem.at[0, slot]),
            pltpu.make_async_copy(bias_hbm.at[b, rows, :], bias_ref.at[rows, :], kv_sem.at[1, slot]),
            pltpu.make_async_copy(vT_hbm.at[b, heads_of_group, :, rows],
                                  vT_ref.at[:, :, rows], kv_sem.at[2, slot]),
        )

    @pl.when(qi == 0)
    def _():
        for copy in key_chunk_copies(0):
            copy.start()

    @pl.when(qi + 1 < pl.num_programs(2))
    def _():
        for copy in key_chunk_copies(qi + 1):
            copy.start()

    for copy in key_chunk_copies(qi):
        copy.wait()

    if online_max:
        m_ref[...] = jnp.full_like(m_ref, -jnp.inf)
    acc_ref[...] = jnp.zeros_like(acc_ref)

    def logits(unit):
        g, k0, diag_offset = unit
        ka = jnp.concatenate([k_ref[g, pl.ds(k0, tk), :], bias_ref[pl.ds(k0, tk), :]],
                             axis=1)
        return _dot(ka, qT_ref[0, g, :, (diag_offset or 0):])

    def softmax_step(unit, s):
        g, k0, diag_offset = unit
        lo = diag_offset or 0
        if diag_offset is not None:
            causal = (lax.broadcasted_iota(jnp.int32, s.shape, 0)
                      <= lax.broadcasted_iota(jnp.int32, s.shape, 1))
            s = jnp.where(causal, s, NEG_INF)
        vt = vT_ref[g, :, pl.ds(k0, tk)]
        if online_max:
            m_prev = m_ref[g, :, lo:]
            m_new = jnp.maximum(m_prev, jnp.max(s, axis=0, keepdims=True))
            alpha = jnp.exp2(m_prev - m_new)
            acc_ref[g, :, lo:] = (alpha * acc_ref[g, :, lo:]
                                  + _dot(vt, jnp.exp2(s - m_new).astype(BF16)))
            m_ref[g, :, lo:] = m_new
        else:
            acc_ref[g, :, lo:] += _dot(vt, jnp.exp2(s).astype(BF16))

    def first_units(k0):
        return [(g, k0, None) for g in range(ATTN_LOOKAHEAD)]

    def run(units, following):
        pending = [s_ref[j] for j in range(ATTN_LOOKAHEAD)]
        seq = units + following
        for i, u in enumerate(units):
            s = pending.pop(0)
            if i + ATTN_LOOKAHEAD < len(seq):
                pending.append(logits(seq[i + ATTN_LOOKAHEAD]))
            softmax_step(u, s)
        for j, s in enumerate(pending):
            s_ref[j] = s

    def full_keys(base, steps):
        run([(g, pl.multiple_of(base + j * tk, tk), None)
             for j in range(steps) for g in range(heads)],
            first_units(pl.multiple_of(base + steps * tk, tk)))

    tile_steps = t // tk

    def body(ki, c):
        full_keys(ki * (2 * t), 2 * tile_steps)
        return c

    for j, u in enumerate(first_units(0)):
        s_ref[j] = logits(u)
    lax.fori_loop(0, qi // 2, body, 0)

    @pl.when(qi % 2 == 1)
    def _():
        full_keys((qi - 1) * t, tile_steps)

    run([(g, pl.multiple_of(q0 + d * tk, tk), d * tk)
         for d in range(t // tk) for g in range(heads)], [])
    for g in range(heads):
        out = acc_ref[g, :HEAD_DIM, :] / acc_ref[g, HEAD_DIM:HEAD_DIM + 1, :]
        o_ref[0, :, g * HEAD_DIM:(g + 1) * HEAD_DIM] = out.T.astype(BF16)


def _fox_attention(qT, k, bias, vT, online_max):
    batch, heads, _, seq = qT.shape
    t = ATTN_TILE
    hg = ATTN_HEADS
    assert ATTN_LOOKAHEAD <= hg and t % ATTN_KEY_TILE == 0
    hbm = pl.BlockSpec(memory_space=pl.ANY)
    return pl.pallas_call(
        functools.partial(_fox_kernel, online_max=online_max),
        grid=(batch, heads // hg, seq // t),
        in_specs=[
            pl.BlockSpec((1, hg, 2 * HEAD_DIM, t), lambda b, h, q: (b, h, 0, q)),
            hbm, hbm, hbm,
        ],
        out_specs=pl.BlockSpec((1, t, hg * HEAD_DIM), lambda b, h, q: (b, q, h)),
        out_shape=jax.ShapeDtypeStruct((batch, seq, heads * HEAD_DIM), BF16),
        scratch_shapes=[
            pltpu.VMEM((hg, 1, t), F32),
            pltpu.VMEM((hg, V_ROWS, t), F32),
            pltpu.VMEM((ATTN_LOOKAHEAD, ATTN_KEY_TILE, t), F32),
            pltpu.VMEM((hg, seq, HEAD_DIM), BF16),
            pltpu.VMEM((seq, V7X_LANES), BF16),
            pltpu.VMEM((hg, V_ROWS, seq), BF16),
            pltpu.SemaphoreType.DMA((3, 2)),
        ],
        compiler_params=_params(3),
        name="fox_attention_online" if online_max else "fox_attention_bounded",
    )(qT, k, bias, vT)


def kernel(x, mem, ffn_norm, ffn_w_gate_up, ffn_w_down, mix_norm, mem_norm, mem_w_kv,
           a_w_in, a_conv_w, a_w_out, kv_norm, w_kvf, b_f, b_w_q, b_w_out, final_norm):
    batch, seq, d = x.shape
    assert d == D_MODEL and seq % TOKEN_TILE == 0 and seq % ATTN_TILE == 0
    assert mem.shape == (batch, MEM_LEN, D_MODEL)
    tokens = batch * seq

    wgu = ffn_w_gate_up.astype(BF16)
    wd = ffn_w_down.astype(BF16)
    w_mem = mem_w_kv.astype(BF16)
    w_in = a_w_in.astype(BF16)
    w_out_a = a_w_out.astype(BF16)
    w_kv = w_kvf[:, :2 * FOX_WIDTH].astype(BF16)
    w_f = jnp.pad(w_kvf[:, 2 * FOX_WIDTH:], ((0, 0), (0, V7X_LANES - FOX_HEADS))).astype(BF16)
    b_f_row = jnp.pad(b_f, (0, V7X_LANES - FOX_HEADS)).reshape(1, V7X_LANES)
    w_q = b_w_q.astype(BF16)
    w_out_b = b_w_out.astype(BF16)
    fg = final_norm.reshape(1, D_MODEL)

    mkT, mv = _mem_kv(mem.reshape(batch * MEM_LEN, D_MODEL), mem_norm.reshape(1, D_MODEL),
                      w_mem, batch)

    h = x.reshape(tokens, D_MODEL)
    k_sh = bias_sh = vT = c_sh = k_max = None
    for l in range(DEPTH):
        if l == N_A:
            k_sh, bias_sh, vT, c_sh, kn = _kvf(h, kv_norm.reshape(1, D_MODEL), w_kv, w_f, b_f_row,
                                               batch, seq)
            k_max = jnp.sqrt(jnp.max(kn[:, 0, :FOX_HEADS].reshape(batch, -1, FOX_HEADS), axis=1))
            k_max = jnp.broadcast_to(k_max[:, :, None], (batch, FOX_HEADS, WIDE_TOKEN_TILE))
        h = _ffn(h, ffn_norm[l, 0].reshape(1, D_MODEL), wgu, wd, (l, 0), fg, False)
        g_mix = mix_norm[l].reshape(1, D_MODEL)
        mix = None
        if l < N_A:
            h = _mixer_a(h, g_mix, w_in, a_conv_w, w_out_a, mkT, mv, l, seq)
        else:
            j = l - N_A
            qT, y_mem, gap = _qproj(h, g_mix, w_q, mkT, mv, c_sh, k_max, l, batch, seq)
            y_tok = lax.cond(
                jnp.max(gap) < MAX_SHIFT_GAP,
                functools.partial(_fox_attention, online_max=False),
                functools.partial(_fox_attention, online_max=True),
                qT, k_sh, bias_sh, vT).reshape(tokens, FOX_WIDTH)
            mix = (y_tok, y_mem, w_out_b, j)
        h = _ffn(h, ffn_norm[l, 1].reshape(1, D_MODEL), wgu, wd, (l, 1), fg, l == DEPTH - 1, mix)
    return h.reshape(batch, seq, D_MODEL)
```

```python
import functools
import math

import jax
import jax.numpy as jnp
import numpy as np
from jax import lax
from jax.experimental import pallas as pl
from jax.experimental.pallas import tpu as pltpu

D_MODEL = 1024
DEPTH = 4
N_A = DEPTH // 2
MEM_LEN = 256
MEM_HEADS = 4
HEAD_DIM = 128
MEM_WIDTH = MEM_HEADS * HEAD_DIM
CONV_DIM = D_MODEL
CONV_WIDTH = 3
FOX_HEADS = 8
FOX_WIDTH = FOX_HEADS * HEAD_DIM
D_FF = 2816
RMS_EPS = 1e-6
NEG_INF = float(np.finfo(np.float32).min)
ATTN_SCALE = HEAD_DIM ** -0.5
LOG2E = math.log2(math.e)

V7X_LANES = 128
V7X_SUBLANES = 8
V7X_MXU_DIM = 256
V7X_VMEM_LIMIT_BYTES = 60000 * 1024

TOKEN_TILE = 512
WIDE_TOKEN_TILE = 1024
FF_CHUNK = V7X_MXU_DIM
ATTN_TILE = 512
ATTN_KEY_TILE = 256
ATTN_LOOKAHEAD = 2
ATTN_HEADS = 8
N_SPLIT = 3
V_PAD_ROWS = 16
V_ROWS = HEAD_DIM + V_PAD_ROWS
AUG_SHIFT = N_SPLIT * FOX_HEADS
AUG_ROWS = 32
assert AUG_SHIFT + N_SPLIT <= AUG_ROWS <= HEAD_DIM
BOUND_MARGIN = 1.02
MAX_SHIFT_GAP = 100.0

F32 = jnp.float32
BF16 = jnp.bfloat16


def _dot(a, b):
    return jnp.dot(a, b, preferred_element_type=F32)


def _rmsnorm(x, g):
    y = x * lax.rsqrt(jnp.mean(x * x, axis=-1, keepdims=True) + RMS_EPS)
    return y * g


def _split_bf16(x):
    pieces = []
    r = x
    for _ in range(N_SPLIT):
        p = r.astype(BF16)
        pieces.append(p)
        r = r - p.astype(F32)
    return pieces


def _head_sum_matrix():
    m = np.zeros((FOX_WIDTH, V7X_LANES), np.float32)
    for hh in range(FOX_HEADS):
        m[hh * HEAD_DIM:(hh + 1) * HEAD_DIM, hh] = 1.0
    return jnp.asarray(m, BF16)


def _params(n_grid_dims):
    return pltpu.CompilerParams(
        dimension_semantics=("arbitrary",) * n_grid_dims,
        vmem_limit_bytes=V7X_VMEM_LIMIT_BYTES)


def _resident(shape):
    zeros = (0,) * len(shape)
    return pl.BlockSpec(shape, lambda *_: zeros, pipeline_mode=pl.Buffered(1))


def _resident_layer(stacked, index):
    n = len(index)
    block_index = tuple(index) + (0,) * (stacked.ndim - n)
    return pl.BlockSpec((None,) * n + tuple(stacked.shape[n:]), lambda *_: block_index,
                        pipeline_mode=pl.Buffered(1))


def _mem_kv_kernel(mem_ref, g_ref, w_ref, mkT_ref, mv_ref, *, batch):
    h = _rmsnorm(mem_ref[...], g_ref[...]).astype(BF16)
    mkv = _dot(h, w_ref[0])
    for b in range(batch):
        blk = mkv[b * MEM_LEN:(b + 1) * MEM_LEN]
        mkT_ref[0, b] = blk[:, :MEM_WIDTH].T.astype(BF16)
        mv_ref[0, b] = blk[:, MEM_WIDTH:].astype(BF16)


def _mem_kv(mem2d, mem_norm, w_kv, batch):
    rows = mem2d.shape[0]
    return pl.pallas_call(
        functools.partial(_mem_kv_kernel, batch=batch),
        grid=(DEPTH,),
        in_specs=[
            pl.BlockSpec((rows, D_MODEL), lambda l: (0, 0)),
            pl.BlockSpec((1, D_MODEL), lambda l: (0, 0)),
            pl.BlockSpec((1, D_MODEL, 2 * MEM_WIDTH), lambda l: (l, 0, 0)),
        ],
        out_specs=[
            pl.BlockSpec((1, batch, MEM_WIDTH, MEM_LEN), lambda l: (l, 0, 0, 0)),
            pl.BlockSpec((1, batch, MEM_LEN, MEM_WIDTH), lambda l: (l, 0, 0, 0)),
        ],
        out_shape=[
            jax.ShapeDtypeStruct((DEPTH, batch, MEM_WIDTH, MEM_LEN), BF16),
            jax.ShapeDtypeStruct((DEPTH, batch, MEM_LEN, MEM_WIDTH), BF16),
        ],
        compiler_params=_params(1),
        name="mem_kv",
    )(mem2d, mem_norm, w_kv)


def _mem_attn(qm, mkT_ref, mv_ref, fillers=()):
    fillers = list(fillers)

    def logits(hh):
        sl = slice(hh * HEAD_DIM, (hh + 1) * HEAD_DIM)
        return _dot(qm[:, sl].astype(BF16), mkT_ref[0, 0, sl, :]) * ATTN_SCALE

    outs = []
    s_next = logits(0)
    for hh in range(MEM_HEADS):
        s = s_next
        if hh + 1 < MEM_HEADS:
            s_next = logits(hh + 1)
        if fillers:
            fillers.pop(0)()
        p = jnp.exp(s - jnp.max(s, axis=-1, keepdims=True))
        denom = jnp.sum(p, axis=-1, keepdims=True)
        outs.append(_dot(p.astype(BF16), mv_ref[0, 0, :, hh * HEAD_DIM:(hh + 1) * HEAD_DIM]) / denom)
    for f in fillers:
        f()
    return jnp.concatenate(outs, axis=-1).astype(BF16)


def _ffn_kernel(*refs, final, mixed):
    if mixed:
        x_ref, yt_ref, ym_ref, wout_ref, g_ref, wgu_ref, wd_ref, fg_ref, o_ref, act_ref = refs
        x = (x_ref[...] + _dot(yt_ref[...], wout_ref[:FOX_WIDTH, :])
             + _dot(ym_ref[...], wout_ref[FOX_WIDTH:, :]))
    else:
        x_ref, g_ref, wgu_ref, wd_ref, fg_ref, o_ref, act_ref = refs
        x = x_ref[...]
    h = _rmsnorm(x, g_ref[...]).astype(BF16)
    for c in range(D_FF // FF_CHUNK):
        lo = c * FF_CHUNK
        gate = _dot(h, wgu_ref[:, lo:lo + FF_CHUNK])
        up = _dot(h, wgu_ref[:, D_FF + lo:D_FF + lo + FF_CHUNK])
        act_ref[:, lo:lo + FF_CHUNK] = (gate * jax.nn.sigmoid(gate) * up).astype(BF16)
    y = x + 0.5 * _dot(act_ref[...], wd_ref[...])
    if final:
        y = _rmsnorm(y, fg_ref[...])
    o_ref[...] = y


def _ffn(x2d, g, wgu, wd, which, fg, final, mix=None):
    tokens = x2d.shape[0]
    tm = WIDE_TOKEN_TILE
    row = lambda i: (i, 0)
    weights = [_resident((1, D_MODEL)), _resident_layer(wgu, which),
               _resident_layer(wd, which), _resident((1, D_MODEL))]
    if mix:
        y_tok, y_mem, w_out, j = mix
        operands = (x2d, y_tok, y_mem, w_out, g, wgu, wd, fg)
        in_specs = [pl.BlockSpec((tm, D_MODEL), row), pl.BlockSpec((tm, FOX_WIDTH), row),
                    pl.BlockSpec((tm, MEM_WIDTH), row), _resident_layer(w_out, (j,))] + weights
    else:
        operands = (x2d, g, wgu, wd, fg)
        in_specs = [pl.BlockSpec((tm, D_MODEL), row)] + weights
    return pl.pallas_call(
        functools.partial(_ffn_kernel, final=final, mixed=bool(mix)),
        grid=(tokens // tm,),
        in_specs=in_specs,
        out_specs=pl.BlockSpec((tm, D_MODEL), row),
        out_shape=jax.ShapeDtypeStruct((tokens, D_MODEL), F32),
        scratch_shapes=[pltpu.VMEM((tm, D_FF), BF16)],
        compiler_params=_params(1),
        name=("ffn_mixed" if mix else "ffn") + ("_final" if final else ""),
    )(*operands)


def _mixer_a_kernel(x_ref, g_ref, win_ref, cw_ref, wout_ref, mkT_ref, mv_ref,
                    o_ref, tail_ref, *, tiles_per_seq):
    tm = x_ref.shape[0]
    c = CONV_DIM

    @pl.when(pl.program_id(0) % tiles_per_seq == 0)
    def _():
        tail_ref[...] = jnp.zeros_like(tail_ref)

    x = x_ref[...]
    h = _rmsnorm(x, g_ref[...]).astype(BF16)
    proj = {}

    def project(name, lo):
        def run():
            proj[name] = _dot(h, win_ref[:, lo:lo + c])
        return run

    y_mem = _mem_attn(_dot(h, win_ref[:, 3 * c:]), mkT_ref, mv_ref,
                      [project("gate_c", c), project("u", 2 * c), project("gate_b", 0)])
    v = proj["gate_c"] * proj["u"]

    prev = tail_ref[...]
    sub = lax.broadcasted_iota(jnp.int32, (V7X_SUBLANES, c), 0)
    shifted = []
    for k in range(1, CONV_WIDTH):
        rv = pltpu.roll(v, k, 0)
        head = jnp.where(sub < k, pltpu.roll(prev, k, 0), rv[:V7X_SUBLANES])
        shifted.append(jnp.concatenate([head, rv[V7X_SUBLANES:]], axis=0))
    tail_ref[...] = v[tm - V7X_SUBLANES:]

    cw = cw_ref[...]
    conv = cw[CONV_WIDTH - 1:CONV_WIDTH] * v
    for k in range(1, CONV_WIDTH):
        conv = conv + cw[CONV_WIDTH - 1 - k:CONV_WIDTH - k] * shifted[k - 1]
    y_tok = (proj["gate_b"] * conv).astype(BF16)
    o_ref[...] = x + _dot(y_tok, wout_ref[:c, :]) + _dot(y_mem, wout_ref[c:, :])


def _mixer_a(x2d, g, w_in, conv_w, w_out, mkT, mv, layer, seq):
    tokens = x2d.shape[0]
    tm = TOKEN_TILE
    tps = seq // tm
    row = lambda i: (i, 0)
    return pl.pallas_call(
        functools.partial(_mixer_a_kernel, tiles_per_seq=tps),
        grid=(tokens // tm,),
        in_specs=[
            pl.BlockSpec((tm, D_MODEL), row),
            _resident((1, D_MODEL)),
            _resident_layer(w_in, (layer,)),
            _resident_layer(conv_w, (layer,)),
            _resident_layer(w_out, (layer,)),
            pl.BlockSpec((1, 1, MEM_WIDTH, MEM_LEN), lambda i: (layer, i // tps, 0, 0)),
            pl.BlockSpec((1, 1, MEM_LEN, MEM_WIDTH), lambda i: (layer, i // tps, 0, 0)),
        ],
        out_specs=pl.BlockSpec((tm, D_MODEL), row),
        out_shape=jax.ShapeDtypeStruct((tokens, D_MODEL), F32),
        scratch_shapes=[pltpu.VMEM((V7X_SUBLANES, CONV_DIM), F32)],
        compiler_params=_params(1),
        name="mixer_a",
    )(x2d, g, w_in, conv_w, w_out, mkT, mv)


def _kvf_kernel(x_ref, g_ref, wkv_ref, wf_ref, bf_ref, tri_ref, sel_ref, hsum_ref,
                k_ref, bias_ref, vT_ref, c_ref, kn_ref, carry_ref, *, tiles_per_seq):
    tm = x_ref.shape[0]

    @pl.when(pl.program_id(0) % tiles_per_seq == 0)
    def _():
        carry_ref[...] = jnp.zeros_like(carry_ref)

    h = _rmsnorm(x_ref[...], g_ref[...]).astype(BF16)

    z = _dot(h, wf_ref[...]) + bf_ref[...]
    ls = -(jnp.maximum(-z, 0.0) + jnp.log1p(jnp.exp(-jnp.abs(z))))
    k = _dot(h, wkv_ref[:, :FOX_WIDTH])
    for hh in range(FOX_HEADS):
        k_ref[0, hh] = k[:, hh * HEAD_DIM:(hh + 1) * HEAD_DIM].astype(BF16)

    cs = _dot(tri_ref[...], jnp.concatenate(_split_bf16(ls), axis=1))
    cum = carry_ref[0:1, :]
    for p in range(N_SPLIT):
        cum = cum + cs[:, p * V7X_LANES:(p + 1) * V7X_LANES]
    carry_ref[...] = jnp.broadcast_to(cum[tm - 1:tm, :], carry_ref.shape)
    c_ref[0] = cum
    kn_ref[0] = jnp.broadcast_to(
        jnp.max(_dot((k * k).astype(BF16), hsum_ref[...]), axis=0, keepdims=True),
        kn_ref.shape[1:])
    v = _dot(h, wkv_ref[:, FOX_WIDTH:])

    bias = _dot(jnp.concatenate(_split_bf16(-LOG2E * cum), axis=1), sel_ref[...])
    lane = lax.broadcasted_iota(jnp.int32, (1, V7X_LANES), 1)
    ones_lanes = jnp.where((lane >= AUG_SHIFT) & (lane < AUG_SHIFT + N_SPLIT), 1.0, 0.0)
    bias_ref[0] = (bias + ones_lanes).astype(BF16)
    for hh in range(FOX_HEADS):
        sl = slice(hh * HEAD_DIM, (hh + 1) * HEAD_DIM)
        vT_ref[0, hh, :HEAD_DIM, :] = v[:, sl].T.astype(BF16)
    denom_rows = lax.broadcasted_iota(jnp.int32, (V_PAD_ROWS, tm), 0) == 0
    for hh in range(FOX_HEADS):
        vT_ref[0, hh, HEAD_DIM:, :] = jnp.where(denom_rows, 1.0, 0.0).astype(BF16)


def _kvf(x2d, g, w_kv, w_f, b_f, batch, seq):
    tokens = x2d.shape[0]
    tm = TOKEN_TILE
    tps = seq // tm
    tri = jnp.tril(jnp.ones((tm, tm), F32)).astype(BF16)
    sel = np.zeros((N_SPLIT * V7X_LANES, V7X_LANES), np.float32)
    for p in range(N_SPLIT):
        for hh in range(FOX_HEADS):
            sel[p * V7X_LANES + hh, N_SPLIT * hh + p] = 1.0
    sel = jnp.asarray(sel, BF16)
    hsum = _head_sum_matrix()
    return pl.pallas_call(
        functools.partial(_kvf_kernel, tiles_per_seq=tps),
        grid=(tokens // tm,),
        in_specs=[
            pl.BlockSpec((tm, D_MODEL), lambda i: (i, 0)),
            _resident((1, D_MODEL)),
            _resident(w_kv.shape),
            _resident(w_f.shape),
            _resident(b_f.shape),
            _resident(tri.shape),
            _resident(sel.shape),
            _resident(hsum.shape),
        ],
        out_specs=[
            pl.BlockSpec((1, FOX_HEADS, tm, HEAD_DIM), lambda i: (i // tps, 0, i % tps, 0)),
            pl.BlockSpec((1, tm, V7X_LANES), lambda i: (i // tps, i % tps, 0)),
            pl.BlockSpec((1, FOX_HEADS, V_ROWS, tm), lambda i: (i // tps, 0, 0, i % tps)),
            pl.BlockSpec((1, tm, V7X_LANES), lambda i: (i // tps, i % tps, 0)),
            pl.BlockSpec((1, V7X_SUBLANES, V7X_LANES), lambda i: (i, 0, 0)),
        ],
        out_shape=[
            jax.ShapeDtypeStruct((batch, FOX_HEADS, seq, HEAD_DIM), BF16),
            jax.ShapeDtypeStruct((batch, seq, V7X_LANES), BF16),
            jax.ShapeDtypeStruct((batch, FOX_HEADS, V_ROWS, seq), BF16),
            jax.ShapeDtypeStruct((batch, seq, V7X_LANES), F32),
            jax.ShapeDtypeStruct((tokens // tm, V7X_SUBLANES, V7X_LANES), F32),
        ],
        scratch_shapes=[pltpu.VMEM((V7X_SUBLANES, V7X_LANES), F32)],
        compiler_params=_params(1),
        name="kvf",
    )(x2d, g, w_kv, w_f, b_f, tri, sel, hsum)


def _qproj_kernel(x_ref, g_ref, wq_ref, mkT_ref, mv_ref, c_ref, kmax_ref,
                  qT_ref, ymem_ref, gap_ref):
    tm = x_ref.shape[0]
    h = _rmsnorm(x_ref[...], g_ref[...]).astype(BF16)
    cT = c_ref[0].T
    row = lax.broadcasted_iota(jnp.int32, (AUG_ROWS, tm), 0)
    gaps = []

    def finish_head(hh, q):
        qT = q.T
        reach =(BOUND_MARGIN * jnp.sqrt(jnp.sum(qT * qT, axis=0, keepdims=True))
                 * kmax_ref[0, hh:hh + 1, :])
        gaps.append(2.0 * reach)
        aug = jnp.where((row >= N_SPLIT * hh) & (row < N_SPLIT * (hh + 1)), 1.0, 0.0)
        for p, piece in enumerate(_split_bf16(LOG2E * cT[hh:hh + 1, :] - reach)):
            aug = jnp.where(row == AUG_SHIFT + p, piece.astype(F32), aug)
        qT_ref[0, hh, :HEAD_DIM, :] = qT.astype(BF16)
        qT_ref[0, hh, HEAD_DIM:HEAD_DIM + AUG_ROWS, :] = aug.astype(BF16)
        qT_ref[0, hh, HEAD_DIM + AUG_ROWS:, :] = jnp.zeros((HEAD_DIM - AUG_ROWS, tm), BF16)

    def project_pair(first):
        def run():
            q2 = (_dot(h, wq_ref[:, first * HEAD_DIM:(first + 2) * HEAD_DIM])
                  * (ATTN_SCALE * LOG2E))
            finish_head(first, q2[:, :HEAD_DIM])
            finish_head(first + 1, q2[:, HEAD_DIM:])
        return run

    ymem_ref[...] = _mem_attn(_dot(h, wq_ref[:, FOX_WIDTH:]), mkT_ref, mv_ref,
                              [project_pair(hh) for hh in range(0, FOX_HEADS, 2)])
    gap_ref[0] = jnp.broadcast_to(functools.reduce(jnp.maximum, gaps), gap_ref.shape[1:])


def _qproj(x2d, g, w_q, mkT, mv, c, kmax, layer, batch, seq):
    tokens = x2d.shape[0]
    tm = WIDE_TOKEN_TILE
    tps = seq // tm
    return pl.pallas_call(
        _qproj_kernel,
        grid=(tokens // tm,),
        in_specs=[
            pl.BlockSpec((tm, D_MODEL), lambda i: (i, 0)),
            _resident((1, D_MODEL)),
            _resident_layer(w_q, (layer - N_A,)),
            pl.BlockSpec((1, 1, MEM_WIDTH, MEM_LEN), lambda i: (layer, i // tps, 0, 0)),
            pl.BlockSpec((1, 1, MEM_LEN, MEM_WIDTH), lambda i: (layer, i // tps, 0, 0)),
            pl.BlockSpec((1, tm, V7X_LANES), lambda i: (i // tps, i % tps, 0)),
            pl.BlockSpec((1, FOX_HEADS, tm), lambda i: (i // tps, 0, 0)),
        ],
        out_specs=[
            pl.BlockSpec((1, FOX_HEADS, 2 * HEAD_DIM, tm), lambda i: (i // tps, 0, 0, i % tps)),
            pl.BlockSpec((tm, MEM_WIDTH), lambda i: (i, 0)),
            pl.BlockSpec((1, V7X_SUBLANES, tm), lambda i: (i, 0, 0)),
        ],
        out_shape=[
            jax.ShapeDtypeStruct((batch, FOX_HEADS, 2 * HEAD_DIM, seq), BF16),
            jax.ShapeDtypeStruct((tokens, MEM_WIDTH), BF16),
            jax.ShapeDtypeStruct((tokens // tm, V7X_SUBLANES, tm), F32),
        ],
        compiler_params=_params(1),
        name="qproj",
    )(x2d, g, w_q, mkT, mv, c, kmax)


def _fox_kernel(qT_ref, k_hbm, bias_hbm, vT_hbm, o_ref, m_ref, acc_ref, s_ref,
                k_ref, bias_ref, vT_ref, kv_sem, *, online_max):
    t = ATTN_TILE
    tk = ATTN_KEY_TILE
    heads = qT_ref.shape[1]
    b, hg, qi = pl.program_id(0), pl.program_id(1), pl.program_id(2)
    q0 = pl.multiple_of(qi * t, t)

    def key_chunk_copies(chunk):
        rows = pl.ds(pl.multiple_of(chunk * t, t), t)
        heads_of_group = pl.ds(hg * heads, heads)
        slot = chunk % 2
        return (
            pltpu.make_async_copy(k_hbm.at[b, heads_of_group, rows, :],
                                  k_ref.at[:, rows, :], kv_sem.at[0, slot]),
            pltpu.make_async_copy(bias_hbm.at[b, rows, :], bias_ref.at[rows, :], kv_sem.at[1, slot]),
            pltpu.make_async_copy(vT_hbm.at[b, heads_of_group, :, rows],
                                  vT_ref.at[:, :, rows], kv_sem.at[2, slot]),
        )

    @pl.when(qi == 0)
    def _():
        for copy in key_chunk_copies(0):
            copy.start()

    @pl.when(qi + 1 < pl.num_programs(2))
    def _():
        for copy in key_chunk_copies(qi + 1):
            copy.start()

    for copy in key_chunk_copies(qi):
        copy.wait()

    if online_max:
        m_ref[...] = jnp.full_like(m_ref, -jnp.inf)
    acc_ref[...] = jnp.zeros_like(acc_ref)

    def logits(unit):
        g, k0, diag_offset = unit
        ka = jnp.concatenate([k_ref[g, pl.ds(k0, tk), :], bias_ref[pl.ds(k0, tk), :]],
                             axis=1)
        return _dot(ka, qT_ref[0, g, :, (diag_offset or 0):])

    def softmax_step(unit, s):
        g, k0, diag_offset = unit
        lo = diag_offset or 0
        if diag_offset is not None:
            causal = (lax.broadcasted_iota(jnp.int32, s.shape, 0)
                      <= lax.broadcasted_iota(jnp.int32, s.shape, 1))
            s = jnp.where(causal, s, NEG_INF)
        vt = vT_ref[g, :, pl.ds(k0, tk)]
        if online_max:
            m_prev = m_ref[g, :, lo:]
            m_new = jnp.maximum(m_prev, jnp.max(s, axis=0, keepdims=True))
            alpha = jnp.exp2(m_prev - m_new)
            acc_ref[g, :, lo:] = (alpha * acc_ref[g, :, lo:]
                                  + _dot(vt, jnp.exp2(s - m_new).astype(BF16)))
            m_ref[g, :, lo:] = m_new
        else:
            acc_ref[g, :, lo:] += _dot(vt, jnp.exp2(s).astype(BF16))

    def first_units(k0):
        return [(g, k0, None) for g in range(ATTN_LOOKAHEAD)]

    def run(units, following):
        pending = [s_ref[j] for j in range(ATTN_LOOKAHEAD)]
        seq = units + following
        for i, u in enumerate(units):
            s = pending.pop(0)
            if i + ATTN_LOOKAHEAD < len(seq):
                pending.append(logits(seq[i + ATTN_LOOKAHEAD]))
            softmax_step(u, s)
        for j, s in enumerate(pending):
            s_ref[j] = s

    def full_keys(base, steps):
        run([(g, pl.multiple_of(base + j * tk, tk), None)
             for j in range(steps) for g in range(heads)],
            first_units(pl.multiple_of(base + steps * tk, tk)))

    tile_steps = t // tk

    def body(ki, c):
        full_keys(ki * (2 * t), 2 * tile_steps)
        return c

    for j, u in enumerate(first_units(0)):
        s_ref[j] = logits(u)
    lax.fori_loop(0, qi // 2, body, 0)

    @pl.when(qi % 2 == 1)
    def _():
        full_keys((qi - 1) * t, tile_steps)

    run([(g, pl.multiple_of(q0 + d * tk, tk), d * tk)
         for d in range(t // tk) for g in range(heads)], [])
    for g in range(heads):
        out = acc_ref[g, :HEAD_DIM, :] / acc_ref[g, HEAD_DIM:HEAD_DIM + 1, :]
        o_ref[0, :, g * HEAD_DIM:(g + 1) * HEAD_DIM] = out.T.astype(BF16)


def _fox_attention(qT, k, bias, vT, online_max):
    batch, heads, _, seq = qT.shape
    t = ATTN_TILE
    hg = ATTN_HEADS
    assert ATTN_LOOKAHEAD <= hg and t % ATTN_KEY_TILE == 0
    hbm = pl.BlockSpec(memory_space=pl.ANY)
    return pl.pallas_call(
        functools.partial(_fox_kernel, online_max=online_max),
        grid=(batch, heads // hg, seq // t),
        in_specs=[
            pl.BlockSpec((1, hg, 2 * HEAD_DIM, t), lambda b, h, q: (b, h, 0, q)),
            hbm, hbm, hbm,
        ],
        out_specs=pl.BlockSpec((1, t, hg * HEAD_DIM), lambda b, h, q: (b, q, h)),
        out_shape=jax.ShapeDtypeStruct((batch, seq, heads * HEAD_DIM), BF16),
        scratch_shapes=[
            pltpu.VMEM((hg, 1, t), F32),
            pltpu.VMEM((hg, V_ROWS, t), F32),
            pltpu.VMEM((ATTN_LOOKAHEAD, ATTN_KEY_TILE, t), F32),
            pltpu.VMEM((hg, seq, HEAD_DIM), BF16),
            pltpu.VMEM((seq, V7X_LANES), BF16),
            pltpu.VMEM((hg, V_ROWS, seq), BF16),
            pltpu.SemaphoreType.DMA((3, 2)),
        ],
        compiler_params=_params(3),
        name="fox_attention_online" if online_max else "fox_attention_bounded",
    )(qT, k, bias, vT)


def kernel(x, mem, ffn_norm, ffn_w_gate_up, ffn_w_down, mix_norm, mem_norm, mem_w_kv,
           a_w_in, a_conv_w, a_w_out, kv_norm, w_kvf, b_f, b_w_q, b_w_out, final_norm):
    batch, seq, d = x.shape
    assert d == D_MODEL and seq % TOKEN_TILE == 0 and seq % ATTN_TILE == 0
    assert mem.shape == (batch, MEM_LEN, D_MODEL)
    tokens = batch * seq

    wgu = ffn_w_gate_up.astype(BF16)
    wd = ffn_w_down.astype(BF16)
    w_mem = mem_w_kv.astype(BF16)
    w_in = a_w_in.astype(BF16)
    w_out_a = a_w_out.astype(BF16)
    w_kv = w_kvf[:, :2 * FOX_WIDTH].astype(BF16)
    w_f = jnp.pad(w_kvf[:, 2 * FOX_WIDTH:], ((0, 0), (0, V7X_LANES - FOX_HEADS))).astype(BF16)
    b_f_row = jnp.pad(b_f, (0, V7X_LANES - FOX_HEADS)).reshape(1, V7X_LANES)
    w_q = b_w_q.astype(BF16)
    w_out_b = b_w_out.astype(BF16)
    fg = final_norm.reshape(1, D_MODEL)

    mkT, mv = _mem_kv(mem.reshape(batch * MEM_LEN, D_MODEL), mem_norm.reshape(1, D_MODEL),
                      w_mem, batch)

    h = x.reshape(tokens, D_MODEL)
    k_sh = bias_sh = vT = c_sh = k_max = None
    for l in range(DEPTH):
        if l == N_A:
            k_sh, bias_sh, vT, c_sh, kn = _kvf(h, kv_norm.reshape(1, D_MODEL), w_kv, w_f, b_f_row,
                                               batch, seq)
            k_max = jnp.sqrt(jnp.max(kn[:, 0, :FOX_HEADS].reshape(batch, -1, FOX_HEADS), axis=1))
            k_max = jnp.broadcast_to(k_max[:, :, None], (batch, FOX_HEADS, WIDE_TOKEN_TILE))
        h = _ffn(h, ffn_norm[l, 0].reshape(1, D_MODEL), wgu, wd, (l, 0), fg, False)
        g_mix = mix_norm[l].reshape(1, D_MODEL)
        mix = None
        if l < N_A:
            h = _mixer_a(h, g_mix, w_in, a_conv_w, w_out_a, mkT, mv, l, seq)
        else:
            j = l - N_A
            qT, y_mem, gap = _qproj(h, g_mix, w_q, mkT, mv, c_sh, k_max, l, batch, seq)
            y_tok = lax.cond(
                jnp.max(gap) < MAX_SHIFT_GAP,
                functools.partial(_fox_attention, online_max=False),
                functools.partial(_fox_attention, online_max=True),
                qT, k_sh, bias_sh, vT).reshape(tokens, FOX_WIDTH)
            mix = (y_tok, y_mem, w_out_b, j)
        h = _ffn(h, ffn_norm[l, 1].reshape(1, D_MODEL), wgu, wd, (l, 1), fg, l == DEPTH - 1, mix)
    return h.reshape(batch, seq, D_MODEL)
```

```python
import functools
import math

import jax
import jax.numpy as jnp
import numpy as np
from jax import lax
from jax.experimental import pallas as pl
from jax.experimental.pallas import tpu as pltpu

D_MODEL = 1024
DEPTH = 4
N_A = DEPTH // 2
MEM_LEN = 256
MEM_HEADS = 4
HEAD_DIM = 128
MEM_WIDTH = MEM_HEADS * HEAD_DIM
CONV_DIM = D_MODEL
CONV_WIDTH = 3
FOX_HEADS = 8
FOX_WIDTH = FOX_HEADS * HEAD_DIM
D_FF = 2816
RMS_EPS = 1e-6
NEG_INF = float(np.finfo(np.float32).min)
ATTN_SCALE = HEAD_DIM ** -0.5
LOG2E = math.log2(math.e)

V7X_LANES = 128
V7X_SUBLANES = 8
V7X_MXU_DIM = 256
V7X_VMEM_LIMIT_BYTES = 60000 * 1024

TOKEN_TILE = 512
WIDE_TOKEN_TILE = 1024
FF_CHUNK = V7X_MXU_DIM
ATTN_TILE = 512
ATTN_KEY_TILE = 256
ATTN_LOOKAHEAD = 2
ATTN_HEADS = 8
N_SPLIT = 3
V_PAD_ROWS = 16
V_ROWS = HEAD_DIM + V_PAD_ROWS
AUG_SHIFT = N_SPLIT * FOX_HEADS
AUG_ROWS = 32
assert AUG_SHIFT + N_SPLIT <= AUG_ROWS <= HEAD_DIM
BOUND_MARGIN = 1.02
MAX_SHIFT_GAP = 100.0

F32 = jnp.float32
BF16 = jnp.bfloat16


def _dot(a, b):
    return jnp.dot(a, b, preferred_element_type=F32)


def _rmsnorm(x, g):
    y = x * lax.rsqrt(jnp.mean(x * x, axis=-1, keepdims=True) + RMS_EPS)
    return y * g


def _split_bf16(x):
    pieces = []
    r = x
    for _ in range(N_SPLIT):
        p = r.astype(BF16)
        pieces.append(p)
        r = r - p.astype(F32)
    return pieces


def _head_sum_matrix():
    m = np.zeros((FOX_WIDTH, V7X_LANES), np.float32)
    for hh in range(FOX_HEADS):
        m[hh * HEAD_DIM:(hh + 1) * HEAD_DIM, hh] = 1.0
    return jnp.asarray(m, BF16)


def _params(n_grid_dims):
    return pltpu.CompilerParams(
        dimension_semantics=("arbitrary",) * n_grid_dims,
        vmem_limit_bytes=V7X_VMEM_LIMIT_BYTES)


def _resident(shape):
    zeros = (0,) * len(shape)
    return pl.BlockSpec(shape, lambda *_: zeros, pipeline_mode=pl.Buffered(1))


def _resident_layer(stacked, index):
    n = len(index)
    block_index = tuple(index) + (0,) * (stacked.ndim - n)
    return pl.BlockSpec((None,) * n + tuple(stacked.shape[n:]), lambda *_: block_index,
                        pipeline_mode=pl.Buffered(1))


def _mem_kv_kernel(mem_ref, g_ref, w_ref, mkT_ref, mv_ref, *, batch):
    h = _rmsnorm(mem_ref[...], g_ref[...]).astype(BF16)
    mkv = _dot(h, w_ref[0])
    for b in range(batch):
        blk = mkv[b * MEM_LEN:(b + 1) * MEM_LEN]
        mkT_ref[0, b] = blk[:, :MEM_WIDTH].T.astype(BF16)
        mv_ref[0, b] = blk[:, MEM_WIDTH:].astype(BF16)


def _mem_kv(mem2d, mem_norm, w_kv, batch):
    rows = mem2d.shape[0]
    return pl.pallas_call(
        functools.partial(_mem_kv_kernel, batch=batch),
        grid=(DEPTH,),
        in_specs=[
            pl.BlockSpec((rows, D_MODEL), lambda l: (0, 0)),
            pl.BlockSpec((1, D_MODEL), lambda l: (0, 0)),
            pl.BlockSpec((1, D_MODEL, 2 * MEM_WIDTH), lambda l: (l, 0, 0)),
        ],
        out_specs=[
            pl.BlockSpec((1, batch, MEM_WIDTH, MEM_LEN), lambda l: (l, 0, 0, 0)),
            pl.BlockSpec((1, batch, MEM_LEN, MEM_WIDTH), lambda l: (l, 0, 0, 0)),
        ],
        out_shape=[
            jax.ShapeDtypeStruct((DEPTH, batch, MEM_WIDTH, MEM_LEN), BF16),
            jax.ShapeDtypeStruct((DEPTH, batch, MEM_LEN, MEM_WIDTH), BF16),
        ],
        compiler_params=_params(1),
        name="mem_kv",
    )(mem2d, mem_norm, w_kv)


def _mem_attn(qm, mkT_ref, mv_ref, fillers=()):
    fillers = list(fillers)

    def logits(hh):
        sl = slice(hh * HEAD_DIM, (hh + 1) * HEAD_DIM)
        return _dot(qm[:, sl].astype(BF16), mkT_ref[0, 0, sl, :]) * ATTN_SCALE

    outs = []
    s_next = logits(0)
    for hh in range(MEM_HEADS):
        s = s_next
        if hh + 1 < MEM_HEADS:
            s_next = logits(hh + 1)
        if fillers:
            fillers.pop(0)()
        p = jnp.exp(s - jnp.max(s, axis=-1, keepdims=True))
        denom = jnp.sum(p, axis=-1, keepdims=True)
        outs.append(_dot(p.astype(BF16), mv_ref[0, 0, :, hh * HEAD_DIM:(hh + 1) * HEAD_DIM]) / denom)
    for f in fillers:
        f()
    return jnp.concatenate(outs, axis=-1).astype(BF16)


def _ffn_kernel(*refs, final, mixed):
    if mixed:
        x_ref, yt_ref, ym_ref, wout_ref, g_ref, wgu_ref, wd_ref, fg_ref, o_ref, act_ref = refs
        x = (x_ref[...] + _dot(yt_ref[...], wout_ref[:FOX_WIDTH, :])
             + _dot(ym_ref[...], wout_ref[FOX_WIDTH:, :]))
    else:
        x_ref, g_ref, wgu_ref, wd_ref, fg_ref, o_ref, act_ref = refs
        x = x_ref[...]
    h = _rmsnorm(x, g_ref[...]).astype(BF16)
    for c in range(D_FF // FF_CHUNK):
        lo = c * FF_CHUNK
        gate = _dot(h, wgu_ref[:, lo:lo + FF_CHUNK])
        up = _dot(h, wgu_ref[:, D_FF + lo:D_FF + lo + FF_CHUNK])
        act_ref[:, lo:lo + FF_CHUNK] = (gate * jax.nn.sigmoid(gate) * up).astype(BF16)
    y = x + 0.5 * _dot(act_ref[...], wd_ref[...])
    if final:
        y = _rmsnorm(y, fg_ref[...])
    o_ref[...] = y


def _ffn(x2d, g, wgu, wd, which, fg, final, mix=None):
    tokens = x2d.shape[0]
    tm = WIDE_TOKEN_TILE
    row = lambda i: (i, 0)
    weights = [_resident((1, D_MODEL)), _resident_layer(wgu, which),
               _resident_layer(wd, which), _resident((1, D_MODEL))]
    if mix:
        y_tok, y_mem, w_out, j = mix
        operands = (x2d, y_tok, y_mem, w_out, g, wgu, wd, fg)
        in_specs = [pl.BlockSpec((tm, D_MODEL), row), pl.BlockSpec((tm, FOX_WIDTH), row),
                    pl.BlockSpec((tm, MEM_WIDTH), row), _resident_layer(w_out, (j,))] + weights
    else:
        operands = (x2d, g, wgu, wd, fg)
        in_specs = [pl.BlockSpec((tm, D_MODEL), row)] + weights
    return pl.pallas_call(
        functools.partial(_ffn_kernel, final=final, mixed=bool(mix)),
        grid=(tokens // tm,),
        in_specs=in_specs,
        out_specs=pl.BlockSpec((tm, D_MODEL), row),
        out_shape=jax.ShapeDtypeStruct((tokens, D_MODEL), F32),
        scratch_shapes=[pltpu.VMEM((tm, D_FF), BF16)],
        compiler_params=_params(1),
        name=("ffn_mixed" if mix else "ffn") + ("_final" if final else ""),
    )(*operands)


def _mixer_a_kernel(x_ref, g_ref, win_ref, cw_ref, wout_ref, mkT_ref, mv_ref,
                    o_ref, tail_ref, *, tiles_per_seq):
    tm = x_ref.shape[0]
    c = CONV_DIM

    @pl.when(pl.program_id(0) % tiles_per_seq == 0)
    def _():
        tail_ref[...] = jnp.zeros_like(tail_ref)

    x = x_ref[...]
    h = _rmsnorm(x, g_ref[...]).astype(BF16)
    proj = {}

    def project(name, lo):
        def run():
            proj[name] = _dot(h, win_ref[:, lo:lo + c])
        return run

    y_mem = _mem_attn(_dot(h, win_ref[:, 3 * c:]), mkT_ref, mv_ref,
                      [project("gate_c", c), project("u", 2 * c), project("gate_b", 0)])
    v = proj["gate_c"] * proj["u"]

    prev = tail_ref[...]
    sub = lax.broadcasted_iota(jnp.int32, (V7X_SUBLANES, c), 0)
    shifted = []
    for k in range(1, CONV_WIDTH):
        rv = pltpu.roll(v, k, 0)
        head = jnp.where(sub < k, pltpu.roll(prev, k, 0), rv[:V7X_SUBLANES])
        shifted.append(jnp.concatenate([head, rv[V7X_SUBLANES:]], axis=0))
    tail_ref[...] = v[tm - V7X_SUBLANES:]

    cw = cw_ref[...]
    conv = cw[CONV_WIDTH - 1:CONV_WIDTH] * v
    for k in range(1, CONV_WIDTH):
        conv = conv + cw[CONV_WIDTH - 1 - k:CONV_WIDTH - k] * shifted[k - 1]
    y_tok = (proj["gate_b"] * conv).astype(BF16)
    o_ref[...] = x + _dot(y_tok, wout_ref[:c, :]) + _dot(y_mem, wout_ref[c:, :])


def _mixer_a(x2d, g, w_in, conv_w, w_out, mkT, mv, layer, seq):
    tokens = x2d.shape[0]
    tm = TOKEN_TILE
    tps = seq // tm
    row = lambda i: (i, 0)
    return pl.pallas_call(
        functools.partial(_mixer_a_kernel, tiles_per_seq=tps),
        grid=(tokens // tm,),
        in_specs=[
            pl.BlockSpec((tm, D_MODEL), row),
            _resident((1, D_MODEL)),
            _resident_layer(w_in, (layer,)),
            _resident_layer(conv_w, (layer,)),
            _resident_layer(w_out, (layer,)),
            pl.BlockSpec((1, 1, MEM_WIDTH, MEM_LEN), lambda i: (layer, i // tps, 0, 0)),
            pl.BlockSpec((1, 1, MEM_LEN, MEM_WIDTH), lambda i: (layer, i // tps, 0, 0)),
        ],
        out_specs=pl.BlockSpec((tm, D_MODEL), row),
        out_shape=jax.ShapeDtypeStruct((tokens, D_MODEL), F32),
        scratch_shapes=[pltpu.VMEM((V7X_SUBLANES, CONV_DIM), F32)],
        compiler_params=_params(1),
        name="mixer_a",
    )(x2d, g, w_in, conv_w, w_out, mkT, mv)


def _kvf_kernel(x_ref, g_ref, wkv_ref, wf_ref, bf_ref, tri_ref, sel_ref, hsum_ref,
                k_ref, bias_ref, vT_ref, c_ref, kn_ref, carry_ref, *, tiles_per_seq):
    tm = x_ref.shape[0]

    @pl.when(pl.program_id(0) % tiles_per_seq == 0)
    def _():
        carry_ref[...] = jnp.zeros_like(carry_ref)

    h = _rmsnorm(x_ref[...], g_ref[...]).astype(BF16)

    z = _dot(h, wf_ref[...]) + bf_ref[...]
    ls = -(jnp.maximum(-z, 0.0) + jnp.log1p(jnp.exp(-jnp.abs(z))))
    k = _dot(h, wkv_ref[:, :FOX_WIDTH])
    for hh in range(FOX_HEADS):
        k_ref[0, hh] = k[:, hh * HEAD_DIM:(hh + 1) * HEAD_DIM].astype(BF16)

    cs = _dot(tri_ref[...], jnp.concatenate(_split_bf16(ls), axis=1))
    cum = carry_ref[0:1, :]
    for p in range(N_SPLIT):
        cum = cum + cs[:, p * V7X_LANES:(p + 1) * V7X_LANES]
    carry_ref[...] = jnp.broadcast_to(cum[tm - 1:tm, :], carry_ref.shape)
    c_ref[0] = cum
    kn_ref[0] = jnp.broadcast_to(
        jnp.max(_dot((k * k).astype(BF16), hsum_ref[...]), axis=0, keepdims=True),
        kn_ref.shape[1:])
    v = _dot(h, wkv_ref[:, FOX_WIDTH:])

    bias = _dot(jnp.concatenate(_split_bf16(-LOG2E * cum), axis=1), sel_ref[...])
    lane = lax.broadcasted_iota(jnp.int32, (1, V7X_LANES), 1)
    ones_lanes = jnp.where((lane >= AUG_SHIFT) & (lane < AUG_SHIFT + N_SPLIT), 1.0, 0.0)
    bias_ref[0] = (bias + ones_lanes).astype(BF16)
    for hh in range(FOX_HEADS):
        sl = slice(hh * HEAD_DIM, (hh + 1) * HEAD_DIM)
        vT_ref[0, hh, :HEAD_DIM, :] = v[:, sl].T.astype(BF16)
    denom_rows = lax.broadcasted_iota(jnp.int32, (V_PAD_ROWS, tm), 0) == 0
    for hh in range(FOX_HEADS):
        vT_ref[0, hh, HEAD_DIM:, :] = jnp.where(denom_rows, 1.0, 0.0).astype(BF16)


def _kvf(x2d, g, w_kv, w_f, b_f, batch, seq):
    tokens = x2d.shape[0]
    tm = TOKEN_TILE
    tps = seq // tm
    tri = jnp.tril(jnp.ones((tm, tm), F32)).astype(BF16)
    sel = np.zeros((N_SPLIT * V7X_LANES, V7X_LANES), np.float32)
    for p in range(N_SPLIT):
        for hh in range(FOX_HEADS):
            sel[p * V7X_LANES + hh, N_SPLIT * hh + p] = 1.0
    sel = jnp.asarray(sel, BF16)
    hsum = _head_sum_matrix()
    return pl.pallas_call(
        functools.partial(_kvf_kernel, tiles_per_seq=tps),
        grid=(tokens // tm,),
        in_specs=[
            pl.BlockSpec((tm, D_MODEL), lambda i: (i, 0)),
            _resident((1, D_MODEL)),
            _resident(w_kv.shape),
            _resident(w_f.shape),
            _resident(b_f.shape),
            _resident(tri.shape),
            _resident(sel.shape),
            _resident(hsum.shape),
        ],
        out_specs=[
            pl.BlockSpec((1, FOX_HEADS, tm, HEAD_DIM), lambda i: (i // tps, 0, i % tps, 0)),
            pl.BlockSpec((1, tm, V7X_LANES), lambda i: (i // tps, i % tps, 0)),
            pl.BlockSpec((1, FOX_HEADS, V_ROWS, tm), lambda i: (i // tps, 0, 0, i % tps)),
            pl.BlockSpec((1, tm, V7X_LANES), lambda i: (i // tps, i % tps, 0)),
            pl.BlockSpec((1, V7X_SUBLANES, V7X_LANES), lambda i: (i, 0, 0)),
        ],
        out_shape=[
            jax.ShapeDtypeStruct((batch, FOX_HEADS, seq, HEAD_DIM), BF16),
            jax.ShapeDtypeStruct((batch, seq, V7X_LANES), BF16),
            jax.ShapeDtypeStruct((batch, FOX_HEADS, V_ROWS, seq), BF16),
            jax.ShapeDtypeStruct((batch, seq, V7X_LANES), F32),
            jax.ShapeDtypeStruct((tokens // tm, V7X_SUBLANES, V7X_LANES), F32),
        ],
        scratch_shapes=[pltpu.VMEM((V7X_SUBLANES, V7X_LANES), F32)],
        compiler_params=_params(1),
        name="kvf",
    )(x2d, g, w_kv, w_f, b_f, tri, sel, hsum)


def _qproj_kernel(x_ref, g_ref, wq_ref, mkT_ref, mv_ref, c_ref, kmax_ref,
                  qT_ref, ymem_ref, gap_ref):
    tm = x_ref.shape[0]
    h = _rmsnorm(x_ref[...], g_ref[...]).astype(BF16)
    cT = c_ref[0].T
    row = lax.broadcasted_iota(jnp.int32, (AUG_ROWS, tm), 0)
    gaps = []

    def finish_head(hh, q):
        qT = q.T
        reach =(BOUND_MARGIN * jnp.sqrt(jnp.sum(qT * qT, axis=0, keepdims=True))
                 * kmax_ref[0, hh:hh + 1, :])
        gaps.append(2.0 * reach)
        aug = jnp.where((row >= N_SPLIT * hh) & (row < N_SPLIT * (hh + 1)), 1.0, 0.0)
        for p, piece in enumerate(_split_bf16(LOG2E * cT[hh:hh + 1, :] - reach)):
            aug = jnp.where(row == AUG_SHIFT + p, piece.astype(F32), aug)
        qT_ref[0, hh, :HEAD_DIM, :] = qT.astype(BF16)
        qT_ref[0, hh, HEAD_DIM:HEAD_DIM + AUG_ROWS, :] = aug.astype(BF16)
        qT_ref[0, hh, HEAD_DIM + AUG_ROWS:, :] = jnp.zeros((HEAD_DIM - AUG_ROWS, tm), BF16)

    def project_pair(first):
        def run():
            q2 = (_dot(h, wq_ref[:, first * HEAD_DIM:(first + 2) * HEAD_DIM])
                  * (ATTN_SCALE * LOG2E))
            finish_head(first, q2[:, :HEAD_DIM])
            finish_head(first + 1, q2[:, HEAD_DIM:])
        return run

    ymem_ref[...] = _mem_attn(_dot(h, wq_ref[:, FOX_WIDTH:]), mkT_ref, mv_ref,
                              [project_pair(hh) for hh in range(0, FOX_HEADS, 2)])
    gap_ref[0] = jnp.broadcast_to(functools.reduce(jnp.maximum, gaps), gap_ref.shape[1:])


def _qproj(x2d, g, w_q, mkT, mv, c, kmax, layer, batch, seq):
    tokens = x2d.shape[0]
    tm = WIDE_TOKEN_TILE
    tps = seq // tm
    return pl.pallas_call(
        _qproj_kernel,
        grid=(tokens // tm,),
        in_specs=[
            pl.BlockSpec((tm, D_MODEL), lambda i: (i, 0)),
            _resident((1, D_MODEL)),
            _resident_layer(w_q, (layer - N_A,)),
            pl.BlockSpec((1, 1, MEM_WIDTH, MEM_LEN), lambda i: (layer, i // tps, 0, 0)),
            pl.BlockSpec((1, 1, MEM_LEN, MEM_WIDTH), lambda i: (layer, i // tps, 0, 0)),
            pl.BlockSpec((1, tm, V7X_LANES), lambda i: (i // tps, i % tps, 0)),
            pl.BlockSpec((1, FOX_HEADS, tm), lambda i: (i // tps, 0, 0)),
        ],
        out_specs=[
            pl.BlockSpec((1, FOX_HEADS, 2 * HEAD_DIM, tm), lambda i: (i // tps, 0, 0, i % tps)),
            pl.BlockSpec((tm, MEM_WIDTH), lambda i: (i, 0)),
            pl.BlockSpec((1, V7X_SUBLANES, tm), lambda i: (i, 0, 0)),
        ],
        out_shape=[
            jax.ShapeDtypeStruct((batch, FOX_HEADS, 2 * HEAD_DIM, seq), BF16),
            jax.ShapeDtypeStruct((tokens, MEM_WIDTH), BF16),
            jax.ShapeDtypeStruct((tokens // tm, V7X_SUBLANES, tm), F32),
        ],
        compiler_params=_params(1),
        name="qproj",
    )(x2d, g, w_q, mkT, mv, c, kmax)


def _fox_kernel(qT_ref, k_hbm, bias_hbm, vT_hbm, o_ref, m_ref, acc_ref, s_ref,
                k_ref, bias_ref, vT_ref, kv_sem, *, online_max):
    t = ATTN_TILE
    tk = ATTN_KEY_TILE
    heads = qT_ref.shape[1]
    b, hg, qi = pl.program_id(0), pl.program_id(1), pl.program_id(2)
    q0 = pl.multiple_of(qi * t, t)

    def key_chunk_copies(chunk):
        rows = pl.ds(pl.multiple_of(chunk * t, t), t)
        heads_of_group = pl.ds(hg * heads, heads)
        slot = chunk % 2
        return (
            pltpu.make_async_copy(k_hbm.at[b, heads_of_group, rows, :],
                                  k_ref.at[:, rows, :], kv_sem.at[0, slot]),
            pltpu.make_async_copy(bias_hbm.at[b, rows, :], bias_ref.at[rows, :], kv_sem.at[1, slot]),
            pltpu.make_async_copy(vT_hbm.at[b, heads_of_group, :, rows],
                                  vT_ref.at[:, :, rows], kv_sem.at[2, slot]),
        )

    @pl.when(qi == 0)
    def _():
        for copy in key_chunk_copies(0):
            copy.start()

    @pl.when(qi + 1 < pl.num_programs(2))
    def _():
        for i, copy in enumerate(key_chunk_copies(qi + 1)):
            copy.start(priority=i % 2)

    for copy in key_chunk_copies(qi):
        copy.wait()

    if online_max:
        m_ref[...] = jnp.full_like(m_ref, -jnp.inf)
    acc_ref[...] = jnp.zeros_like(acc_ref)

    def logits(unit):
        g, k0, diag_offset = unit
        ka = jnp.concatenate([k_ref[g, pl.ds(k0, tk), :], bias_ref[pl.ds(k0, tk), :]],
                             axis=1)
        return _dot(ka, qT_ref[0, g, :, (diag_offset or 0):])

    def softmax_step(unit, s):
        g, k0, diag_offset = unit
        lo = diag_offset or 0
        if diag_offset is not None:
            causal = (lax.broadcasted_iota(jnp.int32, s.shape, 0)
                      <= lax.broadcasted_iota(jnp.int32, s.shape, 1))
            s = jnp.where(causal, s, NEG_INF)
        vt = vT_ref[g, :, pl.ds(k0, tk)]
        if online_max:
            m_prev = m_ref[g, :, lo:]
            m_new = jnp.maximum(m_prev, jnp.max(s, axis=0, keepdims=True))
            alpha = jnp.exp2(m_prev - m_new)
            acc_ref[g, :, lo:] = (alpha * acc_ref[g, :, lo:]
                                  + _dot(vt, jnp.exp2(s - m_new).astype(BF16)))
            m_ref[g, :, lo:] = m_new
        else:
            acc_ref[g, :, lo:] += _dot(vt, jnp.exp2(s).astype(BF16))

    def first_units(k0):
        return [(g, k0, None) for g in range(ATTN_LOOKAHEAD)]

    def run(units, following):
        pending = [s_ref[j] for j in range(ATTN_LOOKAHEAD)]
        seq = units + following
        for i, u in enumerate(units):
            s = pending.pop(0)
            if i + ATTN_LOOKAHEAD < len(seq):
                pending.append(logits(seq[i + ATTN_LOOKAHEAD]))
            softmax_step(u, s)
        for j, s in enumerate(pending):
            s_ref[j] = s

    def full_keys(base, steps):
        run([(g, pl.multiple_of(base + j * tk, tk), None)
             for j in range(steps) for g in range(heads)],
            first_units(pl.multiple_of(base + steps * tk, tk)))

    tile_steps = t // tk

    def body(ki, c):
        full_keys(ki * (2 * t), 2 * tile_steps)
        return c

    for j, u in enumerate(first_units(0)):
        s_ref[j] = logits(u)
    lax.fori_loop(0, qi // 2, body, 0)

    @pl.when(qi % 2 == 1)
    def _():
        full_keys((qi - 1) * t, tile_steps)

    run([(g, pl.multiple_of(q0 + d * tk, tk), d * tk)
         for d in range(t // tk) for g in range(heads)], [])
    for g in range(heads):
        out = acc_ref[g, :HEAD_DIM, :] / acc_ref[g, HEAD_DIM:HEAD_DIM + 1, :]
        o_ref[0, :, g * HEAD_DIM:(g + 1) * HEAD_DIM] = out.T.astype(BF16)


def _fox_attention(qT, k, bias, vT, online_max):
    batch, heads, _, seq = qT.shape
    t = ATTN_TILE
    hg = ATTN_HEADS
    assert ATTN_LOOKAHEAD <= hg and t % ATTN_KEY_TILE == 0
    hbm = pl.BlockSpec(memory_space=pl.ANY)
    return pl.pallas_call(
        functools.partial(_fox_kernel, online_max=online_max),
        grid=(batch, heads // hg, seq // t),
        in_specs=[
            pl.BlockSpec((1, hg, 2 * HEAD_DIM, t), lambda b, h, q: (b, h, 0, q)),
            hbm, hbm, hbm,
        ],
        out_specs=pl.BlockSpec((1, t, hg * HEAD_DIM), lambda b, h, q: (b, q, h)),
        out_shape=jax.ShapeDtypeStruct((batch, seq, heads * HEAD_DIM), BF16),
        scratch_shapes=[
            pltpu.VMEM((hg, 1, t), F32),
            pltpu.VMEM((hg, V_ROWS, t), F32),
            pltpu.VMEM((ATTN_LOOKAHEAD, ATTN_KEY_TILE, t), F32),
            pltpu.VMEM((hg, seq, HEAD_DIM), BF16),
            pltpu.VMEM((seq, V7X_LANES), BF16),
            pltpu.VMEM((hg, V_ROWS, seq), BF16),
            pltpu.SemaphoreType.DMA((3, 2)),
        ],
        compiler_params=_params(3),
        name="fox_attention_online" if online_max else "fox_attention_bounded",
    )(qT, k, bias, vT)


def kernel(x, mem, ffn_norm, ffn_w_gate_up, ffn_w_down, mix_norm, mem_norm, mem_w_kv,
           a_w_in, a_conv_w, a_w_out, kv_norm, w_kvf, b_f, b_w_q, b_w_out, final_norm):
    batch, seq, d = x.shape
    assert d == D_MODEL and seq % TOKEN_TILE == 0 and seq % ATTN_TILE == 0
    assert mem.shape == (batch, MEM_LEN, D_MODEL)
    tokens = batch * seq

    wgu = ffn_w_gate_up.astype(BF16)
    wd = ffn_w_down.astype(BF16)
    w_mem = mem_w_kv.astype(BF16)
    w_in = a_w_in.astype(BF16)
    w_out_a = a_w_out.astype(BF16)
    w_kv = w_kvf[:, :2 * FOX_WIDTH].astype(BF16)
    w_f = jnp.pad(w_kvf[:, 2 * FOX_WIDTH:], ((0, 0), (0, V7X_LANES - FOX_HEADS))).astype(BF16)
    b_f_row = jnp.pad(b_f, (0, V7X_LANES - FOX_HEADS)).reshape(1, V7X_LANES)
    w_q = b_w_q.astype(BF16)
    w_out_b = b_w_out.astype(BF16)
    fg = final_norm.reshape(1, D_MODEL)

    mkT, mv = _mem_kv(mem.reshape(batch * MEM_LEN, D_MODEL), mem_norm.reshape(1, D_MODEL),
                      w_mem, batch)

    h = x.reshape(tokens, D_MODEL)
    k_sh = bias_sh = vT = c_sh = k_max = None
    for l in range(DEPTH):
        if l == N_A:
            k_sh, bias_sh, vT, c_sh, kn = _kvf(h, kv_norm.reshape(1, D_MODEL), w_kv, w_f, b_f_row,
                                               batch, seq)
            k_max = jnp.sqrt(jnp.max(kn[:, 0, :FOX_HEADS].reshape(batch, -1, FOX_HEADS), axis=1))
            k_max = jnp.broadcast_to(k_max[:, :, None], (batch, FOX_HEADS, WIDE_TOKEN_TILE))
        h = _ffn(h, ffn_norm[l, 0].reshape(1, D_MODEL), wgu, wd, (l, 0), fg, False)
        g_mix = mix_norm[l].reshape(1, D_MODEL)
        mix = None
        if l < N_A:
            h = _mixer_a(h, g_mix, w_in, a_conv_w, w_out_a, mkT, mv, l, seq)
        else:
            j = l - N_A
            qT, y_mem, gap = _qproj(h, g_mix, w_q, mkT, mv, c_sh, k_max, l, batch, seq)
            y_tok = lax.cond(
                jnp.max(gap) < MAX_SHIFT_GAP,
                functools.partial(_fox_attention, online_max=False),
                functools.partial(_fox_attention, online_max=True),
                qT, k_sh, bias_sh, vT).reshape(tokens, FOX_WIDTH)
            mix = (y_tok, y_mem, w_out_b, j)
        h = _ffn(h, ffn_norm[l, 1].reshape(1, D_MODEL), wgu, wd, (l, 1), fg, l == DEPTH - 1, mix)
    return h.reshape(batch, seq, D_MODEL)
```
